```python
import math
import jax
import jax.numpy as jnp
from jax import lax
import numpy as np

D_MODEL = 1024
BATCH = 32
SEQ = 2048
DEPTH = 2
DEC_BATCH = 32
DEC_SEQ = 16
PAST_LEN = 2048

CHUNK = 64
N_EVEN = (DEPTH + 1) // 2
N_ODD = DEPTH // 2
HEAD_DIM = 64
MIX_WIDTH = D_MODEL
FOX_WIDTH = MIX_WIDTH // 2
FOX_HEADS = FOX_WIDTH // HEAD_DIM
RW_WIDTH = MIX_WIDTH - FOX_WIDTH
RW_HEADS = RW_WIDTH // HEAD_DIM
W_LORA = 64
A_LORA = 64
G_LORA = 128
FOX_COLS = 3 * FOX_WIDTH + FOX_HEADS
RW_OFF_W = 3 * RW_WIDTH
RW_OFF_A = RW_OFF_W + W_LORA
RW_OFF_G = RW_OFF_A + A_LORA
RW_COLS = RW_OFF_G + G_LORA
EVEN_IN = FOX_COLS + RW_COLS
Q_BLOCK = 128
GM_CHUNK = 128
GM_GROUPS = 8
GM_WIDTH = D_MODEL
GM_GDIM = GM_WIDTH // GM_GROUPS
D_FF = 2816
FFN_RES_SCALE = 0.5
RMS_EPS = 1e-6
LN_EPS = 1e-5
GN_EPS = 64e-5

kernel_name = 'fox_rwkv7_gmlp_streaming_step'


def rms_norm(x, g):
    xf = x.astype(jnp.float32)
    y = xf * lax.rsqrt(jnp.mean(xf * xf, axis=-1, keepdims=True) + RMS_EPS)
    return (y * g.astype(jnp.float32)).astype(x.dtype)


def layer_norm(x, g, b):
    xf = x.astype(jnp.float32)
    m = jnp.mean(xf, axis=-1, keepdims=True)
    var = jnp.mean(jnp.square(xf - m), axis=-1, keepdims=True)
    y = (xf - m) * lax.rsqrt(var + LN_EPS)
    return (y * g.astype(jnp.float32) + b.astype(jnp.float32)).astype(x.dtype)


def swiglu(h, w_in, w_out):
    gate, up = jnp.split(h @ w_in, 2, axis=-1)
    return (jax.nn.silu(gate) * up) @ w_out


def macaron_half(x, g_pre, g_post, w_in, w_out):
    return x + FFN_RES_SCALE * rms_norm(swiglu(rms_norm(x, g_pre), w_in, w_out), g_post)


def fox_block(q, k, v, cq, ck, qpos, kpos):
    s = jnp.einsum('bqhd,bkhd->bhqk', q, k).astype(jnp.float32) * (HEAD_DIM ** -0.5)
    bias = jnp.swapaxes(cq, 1, 2)[..., :, None] - jnp.swapaxes(ck, 1, 2)[..., None, :]
    mask = kpos[None, :] <= qpos[:, None]
    p = jax.nn.softmax(jnp.where(mask, s + bias, -jnp.inf), axis=-1)
    return jnp.einsum('bhqk,bkhd->bqhd', p.astype(v.dtype), v)


def fox_prompt(q, k, v, logf):
    t_len = q.shape[1]
    c = jnp.cumsum(logf, axis=1)
    pos = jnp.arange(t_len)
    outs = []
    for i in range(t_len // Q_BLOCK):
        lo, hi = i * Q_BLOCK, (i + 1) * Q_BLOCK
        outs.append(fox_block(q[:, lo:hi], k[:, :hi], v[:, :hi], c[:, lo:hi], c[:, :hi], pos[lo:hi], pos[:hi]))
    return jnp.concatenate(outs, axis=1)


def fox_sample(q, k, v, logf, ck, cv, clogf):
    p_len = ck.shape[1]
    t_len = q.shape[1]
    k_all = jnp.concatenate([ck.astype(k.dtype), k], axis=1)
    v_all = jnp.concatenate([cv.astype(v.dtype), v], axis=1)
    c = jnp.cumsum(jnp.concatenate([clogf.astype(jnp.float32), logf], axis=1), axis=1)
    pos = jnp.arange(p_len + t_len)
    return fox_block(q, k_all, v_all, c[:, p_len:], c, pos[p_len:], pos)


def rwkv7_scan(s0, r, w, k, v, a_, b_):
    def step(s, inp):
        r_t, w_t, k_t, v_t, a_t, b_t = inp
        sa = jnp.einsum('bhij,bhj->bhi', s, a_t)
        s = s * w_t[:, :, None, :] + sa[..., :, None] * b_t[:, :, None, :] + v_t[..., :, None] * k_t[:, :, None, :]
        return s, jnp.einsum('bhij,bhj->bhi', s, r_t)
    xs = tuple(jnp.swapaxes(t, 0, 1) for t in (r, w, k, v, a_, b_))
    s_fin, ys = lax.scan(step, s0, xs)
    return s_fin, jnp.swapaxes(ys, 0, 1)


def even_mixer(h, w_in, fox_bf, rw_mu, rw_w0, rw_w2, rw_a0, rw_a2, rw_g2, rw_kk, rw_ka, rw_rk,
               rw_ln_g, rw_ln_b, w_out, past):
    f32 = jnp.float32
    bsz, t_len, _ = h.shape
    proj = h @ w_in
    pf = proj[..., :FOX_COLS]
    pr = proj[..., FOX_COLS:]
    fh = lambda t: t.reshape(bsz, t_len, FOX_HEADS, HEAD_DIM)
    q = fh(pf[..., :FOX_WIDTH])
    k = fh(pf[..., FOX_WIDTH:2 * FOX_WIDTH])
    v = fh(pf[..., 2 * FOX_WIDTH:3 * FOX_WIDTH])
    logf = jax.nn.log_sigmoid(pf[..., 3 * FOX_WIDTH:].astype(f32) + fox_bf.astype(f32))
    if past is None:
        o_fox = fox_prompt(q, k, v, logf)
        prev = jnp.zeros((bsz, 1, RW_COLS), pr.dtype)
        s0 = jnp.zeros((bsz, RW_HEADS, HEAD_DIM, HEAD_DIM), f32)
    else:
        ck, cv, clogf, s_in, prev = past
        o_fox = fox_sample(q, k, v, logf, ck, cv, clogf)
        s0 = s_in.astype(f32)
    prev_seq = jnp.concatenate([prev.astype(pr.dtype), pr[:, :-1]], axis=1)
    xm = pr + (prev_seq - pr) * rw_mu
    r = xm[..., :RW_WIDTH].astype(f32)
    kx = xm[..., RW_WIDTH:2 * RW_WIDTH].astype(f32)
    vx = xm[..., 2 * RW_WIDTH:RW_OFF_W].astype(f32)
    dw = xm[..., RW_OFF_W:RW_OFF_A]
    da = xm[..., RW_OFF_A:RW_OFF_G]
    dg = xm[..., RW_OFF_G:]
    w_log = -jax.nn.softplus(-(rw_w0 + jnp.tanh(dw) @ rw_w2).astype(f32)) - 0.5
    decay = jnp.exp(-jnp.exp(w_log))
    a = jax.nn.sigmoid((rw_a0 + da @ rw_a2).astype(f32))
    g = (jax.nn.sigmoid(dg) @ rw_g2).astype(f32)
    rh = lambda t: t.reshape(bsz, t_len, RW_HEADS, HEAD_DIM)
    kk = rh(kx * rw_kk.astype(f32))
    kk = kk / jnp.maximum(jnp.linalg.norm(kk, axis=-1, keepdims=True), 1e-12)
    k_mod = rh(kx * (1.0 + (a - 1.0) * rw_ka.astype(f32)))
    r_h = rh(r)
    v_h = rh(vx)
    s_fin, y = rwkv7_scan(s0, r_h, rh(decay), k_mod, v_h, -kk, kk * rh(a))
    ym = jnp.mean(y, axis=-1, keepdims=True)
    yv = jnp.mean(jnp.square(y - ym), axis=-1, keepdims=True)
    y = ((y - ym) * lax.rsqrt(yv + GN_EPS)).reshape(bsz, t_len, RW_WIDTH)
    y = y * rw_ln_g.astype(f32) + rw_ln_b.astype(f32)
    bonus = jnp.sum(r_h * k_mod * rw_rk.astype(f32), axis=-1, keepdims=True) * v_h
    y = (y + bonus.reshape(bsz, t_len, RW_WIDTH)) * g
    mixed = jnp.concatenate([o_fox.reshape(bsz, t_len, FOX_WIDTH), y.astype(h.dtype)], axis=-1)
    out = mixed @ w_out
    new_state = (k, v, logf.astype(h.dtype), s_fin.astype(h.dtype), pr[:, -1:])
    return out, new_state


def odd_mixer(h, w_in, ln_g, ln_b, w_s, b_s, w_out, is_prompt):
    bsz, t_len, _ = h.shape
    u, v = jnp.split(jax.nn.gelu(h @ w_in, approximate=False), 2, axis=-1)
    v = layer_norm(v, ln_g, ln_b)
    cpos = jnp.arange(GM_CHUNK) // CHUNK
    wm = jnp.where(cpos[None, :] <= cpos[:, None], w_s, 0.0).astype(v.dtype)
    if is_prompt:
        vg = v.reshape(bsz, t_len // GM_CHUNK, GM_CHUNK, GM_GROUPS, GM_GDIM)
        sp = jnp.einsum('gts,bnsgc->bntgc', wm, vg) + b_s.T[None, None, :, :, None]
    else:
        vg = v.reshape(bsz, t_len, GM_GROUPS, GM_GDIM)
        sp = jnp.einsum('gts,bsgc->btgc', wm[:, :t_len, :t_len], vg) + b_s[:, :t_len].T[None, :, :, None]
    sp = sp.reshape(bsz, t_len, GM_WIDTH).astype(u.dtype)
    return (u * sp) @ w_out, v


def setup_inputs(seed: int = 0) -> dict:
    key = jax.random.key(seed)
    ks = jax.random.split(key, 32)
    f32 = jnp.float32
    nrm = lambda i, shape, scale: jax.random.normal(ks[i], shape, f32) * scale
    return {
        'x_prompt': nrm(0, (BATCH, SEQ, D_MODEL), 1.0),
        'x_sample': nrm(1, (DEC_BATCH, DEC_SEQ, D_MODEL), 1.0),
        'cache_fox_k': nrm(2, (N_EVEN, DEC_BATCH, PAST_LEN, FOX_HEADS, HEAD_DIM), 1.0),
        'cache_fox_v': nrm(3, (N_EVEN, DEC_BATCH, PAST_LEN, FOX_HEADS, HEAD_DIM), 1.0),
        'cache_fox_logf': jax.nn.log_sigmoid(nrm(4, (N_EVEN, DEC_BATCH, PAST_LEN, FOX_HEADS), 1.0) + 2.0),
        'state_rwkv': nrm(5, (N_EVEN, DEC_BATCH, RW_HEADS, HEAD_DIM, HEAD_DIM), 1.0),
        'state_rwkv_shift': nrm(6, (N_EVEN, DEC_BATCH, 1, RW_COLS), 1.0),
        'norm_g': 1.0 + nrm(7, (DEPTH, 6, D_MODEL), 0.05),
        'ffn_w_in': nrm(8, (DEPTH, 2, D_MODEL, 2 * D_FF), D_MODEL ** -0.5),
        'ffn_w_out': nrm(9, (DEPTH, 2, D_FF, D_MODEL), D_FF ** -0.5),
        'even_w_in': nrm(10, (N_EVEN, D_MODEL, EVEN_IN), D_MODEL ** -0.5),
        'fox_bf': 2.0 + nrm(11, (N_EVEN, FOX_HEADS), 0.5),
        'rw_mu': jax.random.uniform(ks[12], (N_EVEN, RW_COLS), f32, 0.0, 1.0),
        'rw_w0': jax.random.uniform(ks[13], (N_EVEN, RW_WIDTH), f32, -6.0, -1.0),
        'rw_w2': nrm(14, (N_EVEN, W_LORA, RW_WIDTH), W_LORA ** -0.5),
        'rw_a0': nrm(15, (N_EVEN, RW_WIDTH), 0.1),
        'rw_a2': nrm(16, (N_EVEN, A_LORA, RW_WIDTH), A_LORA ** -0.5),
        'rw_g2': nrm(17, (N_EVEN, G_LORA, RW_WIDTH), G_LORA ** -0.5),
        'rw_kk': 0.85 + nrm(18, (N_EVEN, RW_WIDTH), 0.05),
        'rw_ka': 1.0 + nrm(19, (N_EVEN, RW_WIDTH), 0.05),
        'rw_rk': nrm(20, (N_EVEN, RW_HEADS, HEAD_DIM), 0.1),
        'rw_ln_g': 1.0 + nrm(21, (N_EVEN, RW_WIDTH), 0.05),
        'rw_ln_b': nrm(22, (N_EVEN, RW_WIDTH), 0.02),
        'even_w_out': nrm(23, (N_EVEN, MIX_WIDTH, D_MODEL), MIX_WIDTH ** -0.5),
        'gm_w_in': nrm(24, (N_ODD, D_MODEL, 2 * GM_WIDTH), D_MODEL ** -0.5),
        'gm_ln_g': 1.0 + nrm(25, (N_ODD, GM_WIDTH), 0.05),
        'gm_ln_b': nrm(26, (N_ODD, GM_WIDTH), 0.02),
        'gm_w_s': nrm(27, (N_ODD, GM_GROUPS, GM_CHUNK, GM_CHUNK), GM_CHUNK ** -0.5),
        'gm_b_s': 1.0 + nrm(28, (N_ODD, GM_GROUPS, GM_CHUNK), 0.1),
        'gm_w_out': nrm(29, (N_ODD, GM_WIDTH, D_MODEL), GM_WIDTH ** -0.5),
    }


def reference(x_prompt, x_sample, cache_fox_k, cache_fox_v, cache_fox_logf, state_rwkv, state_rwkv_shift,
              norm_g, ffn_w_in, ffn_w_out, even_w_in, fox_bf, rw_mu, rw_w0, rw_w2, rw_a0, rw_a2, rw_g2,
              rw_kk, rw_ka, rw_rk, rw_ln_g, rw_ln_b, even_w_out, gm_w_in, gm_ln_g, gm_ln_b, gm_w_s,
              gm_b_s, gm_w_out):
    xp, xs = x_prompt, x_sample
    fk_p, fv_p, fl_p, rs_p, rsh_p = [], [], [], [], []
    fk_s, fv_s, fl_s, rs_s, rsh_s = [], [], [], [], []
    gv_s = []
    for layer in range(DEPTH):
        j = layer // 2
        ng = norm_g[layer]
        xp = macaron_half(xp, ng[0], ng[1], ffn_w_in[layer, 0], ffn_w_out[layer, 0])
        xs = macaron_half(xs, ng[0], ng[1], ffn_w_in[layer, 0], ffn_w_out[layer, 0])
        if layer % 2 == 0:
            ew = (even_w_in[j], fox_bf[j], rw_mu[j], rw_w0[j], rw_w2[j], rw_a0[j], rw_a2[j], rw_g2[j],
                  rw_kk[j], rw_ka[j], rw_rk[j], rw_ln_g[j], rw_ln_b[j], even_w_out[j])
            mp, stp = even_mixer(rms_norm(xp, ng[2]), *ew, None)
            ms, sts = even_mixer(rms_norm(xs, ng[2]), *ew,
                                 (cache_fox_k[j], cache_fox_v[j], cache_fox_logf[j], state_rwkv[j], state_rwkv_shift[j]))
            for lst, val in zip((fk_p, fv_p, fl_p, rs_p, rsh_p), stp):
                lst.append(val)
            for lst, val in zip((fk_s, fv_s, fl_s, rs_s, rsh_s), sts):
                lst.append(val)
        else:
            ow = (gm_w_in[j], gm_ln_g[j], gm_ln_b[j], gm_w_s[j], gm_b_s[j], gm_w_out[j])
            mp, _ = odd_mixer(rms_norm(xp, ng[2]), *ow, True)
            ms, v_new = odd_mixer(rms_norm(xs, ng[2]), *ow, False)
            gv_s.append(v_new)
        xp = xp + rms_norm(mp, ng[3])
        xs = xs + rms_norm(ms, ng[3])
        xp = macaron_half(xp, ng[4], ng[5], ffn_w_in[layer, 1], ffn_w_out[layer, 1])
        xs = macaron_half(xs, ng[4], ng[5], ffn_w_in[layer, 1], ffn_w_out[layer, 1])
    return (xp, xs,
            jnp.stack(fk_p), jnp.stack(fv_p), jnp.stack(fl_p), jnp.stack(rs_p), jnp.stack(rsh_p),
            jnp.stack(fk_s), jnp.stack(fv_s), jnp.stack(fl_s), jnp.stack(rs_s), jnp.stack(rsh_s),
            jnp.stack(gv_s))
```

```python
import functools
import math

import jax
import jax.numpy as jnp
from jax import lax
from jax.experimental import pallas as pl
from jax.experimental.pallas import tpu as pltpu

F32 = jnp.float32
BF16 = jnp.bfloat16

D_MODEL = 1024
D_FF = 2816
HEAD_DIM = 64
FOX_WIDTH = 512
FOX_HEADS = 8
RW_WIDTH = 512
RW_HEADS = 8
RW_COLS = 1792
RW_OFF_W = 1536
RW_OFF_G = 1664
FOX_COLS = 3 * FOX_WIDTH + FOX_HEADS
GM_WIDTH = 1024
GM_CHUNK = 128
GM_GROUPS = 8
CHUNK = 64
FFN_RES_SCALE = 0.5
RMS_EPS = 1e-6
LN_EPS = 1e-5
GN_EPS = 64e-5

LANE = 128
ROW_TILE = 512
FF_CHUNK = 256
FOX_BLOCK = 256
RW_CHUNK = 64
RW_ROWS = 256
NEG_BIG = -1e30
VMEM_LIMIT = 56 * 1024 * 1024


def _params(sem):
    return pltpu.CompilerParams(dimension_semantics=sem, vmem_limit_bytes=VMEM_LIMIT)


def _const_spec(shape):
    n = len(shape)
    return pl.BlockSpec(shape, lambda *_: (0,) * n, pipeline_mode=pl.Buffered(1))


def _dot(a, b):
    return jnp.dot(a, b, preferred_element_type=F32)


def _dot_nt(a, b):
    return lax.dot_general(a, b, (((1,), (1,)), ((), ())), preferred_element_type=F32)


def _dot_tn(a, b):
    return lax.dot_general(a, b, (((0,), (0,)), ((), ())), preferred_element_type=F32)


def _split2(x):
    hi = x.astype(BF16)
    lo = (x - hi.astype(F32)).astype(BF16)
    return hi, lo


def _split3(x):
    hi = x.astype(BF16)
    r1 = x - hi.astype(F32)
    mid = r1.astype(BF16)
    lo = (r1 - mid.astype(F32)).astype(BF16)
    return hi, mid, lo


def _rms(x, g):
    return x * lax.rsqrt(jnp.mean(x * x, axis=-1, keepdims=True) + RMS_EPS) * g


def _sigmoid(x):
    return 1.0 / (1.0 + jnp.exp(-x))


def _log_sigmoid(z):
    return jnp.minimum(z, 0.0) - jnp.log1p(jnp.exp(-jnp.abs(z)))


def _ffn_apply(x, gpre, gpost, win_ref, wout_ref, act_ref):
    h = _rms(x, gpre).astype(BF16)
    for c in range(D_FF // FF_CHUNK):
        lo = c * FF_CHUNK
        gate = _dot(h, win_ref[:, lo:lo + FF_CHUNK])
        up = _dot(h, win_ref[:, D_FF + lo:D_FF + lo + FF_CHUNK])
        act_ref[:, lo:lo + FF_CHUNK] = (gate * _sigmoid(gate) * up).astype(BF16)
    y = _dot(act_ref[...], wout_ref[...])
    return x + FFN_RES_SCALE * _rms(y, gpost)


def _ffn_kernel(x_ref, gpre_ref, gpost_ref, win_ref, wout_ref, o_ref, act_ref):
    o_ref[...] = _ffn_apply(x_ref[...], gpre_ref[...], gpost_ref[...], win_ref, wout_ref, act_ref)


def _row_spec(tm, width):
    return pl.BlockSpec((tm, width), lambda i: (i, 0))


def _ffn(x, gpre, gpost, win, wout):
    n = x.shape[0]
    tm = min(ROW_TILE, n)
    return pl.pallas_call(
        _ffn_kernel,
        grid=(n // tm,),
        in_specs=[_row_spec(tm, D_MODEL), _const_spec((1, D_MODEL)), _const_spec((1, D_MODEL)),
                  _const_spec((D_MODEL, 2 * D_FF)), _const_spec((D_FF, D_MODEL))],
        out_specs=_row_spec(tm, D_MODEL),
        out_shape=jax.ShapeDtypeStruct((n, D_MODEL), F32),
        scratch_shapes=[pltpu.VMEM((tm, D_FF), BF16)],
        compiler_params=_params(("parallel",)),
        name="ffn",
    )(x, gpre, gpost, win, wout)


def _proj_kernel(x_ref, g_ref, wqkv_ref, wf_ref, bf_ref, wrw_ref,
                 q_ref, k_ref, v_ref, lf_ref, lft_ref, pr_ref):
    h = _rms(x_ref[...], g_ref[...]).astype(BF16)
    q_ref[...] = _dot(h, wqkv_ref[:, :FOX_WIDTH]).astype(BF16)
    k_ref[...] = _dot(h, wqkv_ref[:, FOX_WIDTH:2 * FOX_WIDTH])
    v_ref[...] = _dot(h, wqkv_ref[:, 2 * FOX_WIDTH:])
    lf = _log_sigmoid(_dot(h, wf_ref[...]) + bf_ref[...])
    lf_ref[...] = lf[:, :FOX_HEADS]
    lft_ref[...] = lf.T[:FOX_HEADS, :]
    pr_ref[...] = _dot(h, wrw_ref[...])


def _proj(x, g, wqkv, wf, bf, wrw):
    n = x.shape[0]
    tm = min(ROW_TILE, n)
    return pl.pallas_call(
        _proj_kernel,
        grid=(n // tm,),
        in_specs=[_row_spec(tm, D_MODEL), _const_spec((1, D_MODEL)),
                  _const_spec((D_MODEL, 3 * FOX_WIDTH)), _const_spec((D_MODEL, LANE)),
                  _const_spec((1, LANE)), _const_spec((D_MODEL, RW_COLS))],
        out_specs=[_row_spec(tm, FOX_WIDTH), _row_spec(tm, FOX_WIDTH), _row_spec(tm, FOX_WIDTH),
                   _row_spec(tm, FOX_HEADS), pl.BlockSpec((FOX_HEADS, tm), lambda i: (0, i)),
                   _row_spec(tm, RW_COLS)],
        out_shape=[jax.ShapeDtypeStruct((n, FOX_WIDTH), BF16),
                   jax.ShapeDtypeStruct((n, FOX_WIDTH), F32),
                   jax.ShapeDtypeStruct((n, FOX_WIDTH), F32),
                   jax.ShapeDtypeStruct((n, FOX_HEADS), F32),
                   jax.ShapeDtypeStruct((FOX_HEADS, n), F32),
                   jax.ShapeDtypeStruct((n, RW_COLS), F32)],
        compiler_params=_params(("parallel",)),
        name="even_proj",
    )(x, g, wqkv, wf, bf, wrw)


def _lane_cumsum(rows, carry):
    r = lax.broadcasted_iota(jnp.int32, (LANE, LANE), 0)
    c = lax.broadcasted_iota(jnp.int32, (LANE, LANE), 1)
    upper = (r <= c).astype(BF16)
    hi, mid, lo = _split3(rows)
    return _dot(hi, upper) + _dot(mid, upper) + _dot(lo, upper) + carry


def _fox_prompt_kernel(q_ref, k_ref, v_ref, lft_ref, o_ref, kb_ref, vb_ref, ct_ref, *, seq):
    i = pl.program_id(1)
    bq = FOX_BLOCK

    @pl.when(i == 0)
    def _():
        def cast(c, _):
            r0 = pl.multiple_of(c * bq, bq)
            kb_ref[pl.ds(r0, bq), :] = k_ref[pl.ds(r0, bq), :].astype(BF16)
            vb_ref[pl.ds(r0, bq), :] = v_ref[pl.ds(r0, bq), :].astype(BF16)
            return 0
        lax.fori_loop(0, seq // bq, cast, 0)
        carry = jnp.zeros((FOX_HEADS, 1), F32)
        for blk in range(seq // LANE):
            c = _lane_cumsum(lft_ref[:, blk * LANE:(blk + 1) * LANE], carry)
            ct_ref[:, blk * LANE:(blk + 1) * LANE] = c
            carry = c[:, LANE - 1:LANE]

    q0 = pl.multiple_of(i * bq, bq)
    lane = lax.broadcasted_iota(jnp.int32, (1, LANE), 1)
    row = lax.broadcasted_iota(jnp.int32, (bq, bq), 0)
    col = lax.broadcasted_iota(jnp.int32, (bq, bq), 1)
    causal = col <= row
    for p in range(FOX_HEADS // 2):
        lanes = slice(p * LANE, (p + 1) * LANE)
        qp = q_ref[:, lanes].astype(F32) * (HEAD_DIM ** -0.5)
        outs = []
        for hh in range(2):
            h = 2 * p + hh
            in_head = (lane >= HEAD_DIM) if hh else (lane < HEAD_DIM)
            qm = jnp.where(in_head, qp, 0.0).astype(BF16)
            cq0 = ct_ref[h:h + 1, pl.ds(q0, LANE)][:, 0:1]

            def scores(j0, qm=qm, h=h, cq0=cq0, lanes=lanes):
                s = _dot_nt(qm, kb_ref[pl.ds(j0, bq), lanes])
                return s - (ct_ref[h:h + 1, pl.ds(j0, bq)] - cq0)

            s = jnp.where(causal, scores(q0), NEG_BIG)
            m = jnp.max(s, axis=1, keepdims=True)
            pe = jnp.exp(s - m)
            l = jnp.sum(pe, axis=1, keepdims=True)
            acc = _dot(pe.astype(BF16), vb_ref[pl.ds(q0, bq), lanes])

            def body(j, carry, scores=scores, lanes=lanes):
                m, l, acc = carry
                j0 = pl.multiple_of(j * bq, bq)
                s = scores(j0)
                m_new = jnp.maximum(m, jnp.max(s, axis=1, keepdims=True))
                alpha = jnp.exp(m - m_new)
                pe = jnp.exp(s - m_new)
                l = alpha * l + jnp.sum(pe, axis=1, keepdims=True)
                acc = alpha * acc + _dot(pe.astype(BF16), vb_ref[pl.ds(j0, bq), lanes])
                return m_new, l, acc

            m, l, acc = lax.fori_loop(0, i, body, (m, l, acc))
            outs.append(acc / l)
        o_ref[:, lanes] = jnp.where(lane < HEAD_DIM, outs[0], outs[1]).astype(BF16)


def _fox_prompt(q, k, v, lft, batch, seq):
    nq = seq // FOX_BLOCK
    return pl.pallas_call(
        functools.partial(_fox_prompt_kernel, seq=seq),
        grid=(batch, nq),
        in_specs=[pl.BlockSpec((FOX_BLOCK, FOX_WIDTH), lambda b, i: (b * nq + i, 0)),
                  pl.BlockSpec((seq, FOX_WIDTH), lambda b, i: (b, 0)),
                  pl.BlockSpec((seq, FOX_WIDTH), lambda b, i: (b, 0)),
                  pl.BlockSpec((FOX_HEADS, seq), lambda b, i: (0, b))],
        out_specs=pl.BlockSpec((FOX_BLOCK, FOX_WIDTH), lambda b, i: (b * nq + i, 0)),
        out_shape=jax.ShapeDtypeStruct((batch * seq, FOX_WIDTH), BF16),
        scratch_shapes=[pltpu.VMEM((seq, FOX_WIDTH), BF16), pltpu.VMEM((seq, FOX_WIDTH), BF16),
                        pltpu.VMEM((FOX_HEADS, seq), F32)],
        compiler_params=_params(("parallel", "arbitrary")),
        name="fox_prompt",
    )(q, k, v, lft)


def _fox_sample_kernel(q_ref, kn_ref, vn_ref, lfn_ref, kc_ref, vc_ref, lfc_ref, o_ref,
                       kb_ref, vb_ref, s_ref, *, past, tnew):
    rows = FOX_HEADS * tnew
    q = q_ref[...].astype(F32) * (HEAD_DIM ** -0.5)
    lane_head = lax.broadcasted_iota(jnp.int32, (tnew, FOX_WIDTH), 1) // HEAD_DIM
    qs = jnp.concatenate([jnp.where(lane_head == h, q, 0.0) for h in range(FOX_HEADS)], axis=0).astype(BF16)

    blk = 512
    def cast(c, _):
        r0 = pl.multiple_of(c * blk, blk)
        kb_ref[pl.ds(r0, blk), :] = kc_ref[0, pl.ds(r0, blk), :].astype(BF16)
        vb_ref[pl.ds(r0, blk), :] = vc_ref[0, pl.ds(r0, blk), :].astype(BF16)
        return 0
    lax.fori_loop(0, past // blk, cast, 0)

    carry = jnp.zeros((FOX_HEADS, 1), F32)
    cts = []
    for b in range(past // LANE):
        c = _lane_cumsum(lfc_ref[0, :, b * LANE:(b + 1) * LANE], carry)
        cts.append(c)
        carry = c[:, LANE - 1:LANE]
    total = carry
    r = lax.broadcasted_iota(jnp.int32, (tnew, tnew), 0)
    c = lax.broadcasted_iota(jnp.int32, (tnew, tnew), 1)
    hi, mid, lo = _split3(lfn_ref[0])
    upper = (r <= c).astype(BF16)
    cnew = _dot(hi, upper) + _dot(mid, upper) + _dot(lo, upper)

    for b in range(past // LANE):
        sl = slice(b * LANE, (b + 1) * LANE)
        s = _dot_nt(qs, kb_ref[sl, :])
        ck = cts[b] - total
        bias = jnp.concatenate([jnp.broadcast_to(ck[h:h + 1, :], (tnew, LANE)) for h in range(FOX_HEADS)], axis=0)
        s_ref[:, sl] = s - bias
    sn = _dot_nt(qs, kn_ref[...].astype(BF16))
    bias_n = jnp.concatenate([jnp.broadcast_to(cnew[h:h + 1, :], (tnew, tnew)) for h in range(FOX_HEADS)], axis=0)
    tq = lax.broadcasted_iota(jnp.int32, (rows, tnew), 0) % tnew
    tk = lax.broadcasted_iota(jnp.int32, (rows, tnew), 1)
    sn = jnp.where(tk <= tq, sn - bias_n, NEG_BIG)

    sc = s_ref[...]
    m = jnp.maximum(jnp.max(sc, axis=1, keepdims=True), jnp.max(sn, axis=1, keepdims=True))
    pc = jnp.exp(sc - m)
    pn = jnp.exp(sn - m)
    l = jnp.sum(pc, axis=1, keepdims=True) + jnp.sum(pn, axis=1, keepdims=True)
    o = (_dot(pc.astype(BF16), vb_ref[...]) + _dot(pn.astype(BF16), vn_ref[...].astype(BF16))) / l
    out = jnp.zeros((tnew, FOX_WIDTH), F32)
    for h in range(FOX_HEADS):
        out = out + jnp.where(lane_head == h, o[h * tnew:(h + 1) * tnew, :], 0.0)
    o_ref[...] = out.astype(BF16)


def _fox_sample(q, kn, vn, lfn_t, kc, vc, lfc_t, batch, tnew, past):
    return pl.pallas_call(
        functools.partial(_fox_sample_kernel, past=past, tnew=tnew),
        grid=(batch,),
        in_specs=[pl.BlockSpec((tnew, FOX_WIDTH), lambda b: (b, 0)),
                  pl.BlockSpec((tnew, FOX_WIDTH), lambda b: (b, 0)),
                  pl.BlockSpec((tnew, FOX_WIDTH), lambda b: (b, 0)),
                  pl.BlockSpec((1, FOX_HEADS, tnew), lambda b: (b, 0, 0)),
                  pl.BlockSpec((1, past, FOX_WIDTH), lambda b: (b, 0, 0)),
                  pl.BlockSpec((1, past, FOX_WIDTH), lambda b: (b, 0, 0)),
                  pl.BlockSpec((1, FOX_HEADS, past), lambda b: (b, 0, 0))],
        out_specs=pl.BlockSpec((tnew, FOX_WIDTH), lambda b: (b, 0)),
        out_shape=jax.ShapeDtypeStruct((batch * tnew, FOX_WIDTH), BF16),
        scratch_shapes=[pltpu.VMEM((past, FOX_WIDTH), BF16), pltpu.VMEM((past, FOX_WIDTH), BF16),
                        pltpu.VMEM((FOX_HEADS * tnew, past), F32)],
        compiler_params=_params(("parallel",)),
        name="fox_sample",
    )(q, kn, vn, lfn_t, kc, vc, lfc_t)


def _stack_pair(x):
    lane = lax.broadcasted_iota(jnp.int32, x.shape, 1)
    return jnp.concatenate([jnp.where(lane < HEAD_DIM, x, 0.0), jnp.where(lane >= HEAD_DIM, x, 0.0)], axis=0)


def _seg_sum(z, e_ref):
    hi, lo = _split2(z)
    return _dot(hi, e_ref[...]) + _dot(lo, e_ref[...])


def _rwkv_kernel(pr_ref, prev_ref, s0_ref, mu_ref, w0_ref, w2h_ref, w2l_ref, a0_ref, a2_ref, g2_ref,
                 kk_ref, ka_ref, rk_ref, lng_ref, lnb_ref, e_ref,
                 y_ref, sfin_ref, s_ref, pv_ref, *, n_chunks, t_valid):
    cc = RW_CHUNK
    t = pl.program_id(1)

    @pl.when(t == 0)
    def _():
        s_ref[...] = s0_ref[0]
        pv_ref[...] = prev_ref[0]

    ri = lax.broadcasted_iota(jnp.int32, (cc, cc), 0)
    ci = lax.broadcasted_iota(jnp.int32, (cc, cc), 1)
    tri = (ci <= ri).astype(BF16)
    r2 = lax.broadcasted_iota(jnp.int32, (2 * cc, 2 * cc), 0) % cc
    c2 = lax.broadcasted_iota(jnp.int32, (2 * cc, 2 * cc), 1) % cc
    strict = c2 < r2
    incl = c2 <= r2
    row1 = lax.broadcasted_iota(jnp.int32, (cc, 1), 0)

    def chunk(c, _):
        r0 = pl.multiple_of(c * cc, cc)
        pr = pr_ref[0, pl.ds(r0, cc), :]
        prev_seq = jnp.where(row1 == 0, pv_ref[...], pltpu.roll(pr, 1, 0))
        pv_ref[...] = pr[cc - 1:cc, :]
        xm = pr + (prev_seq - pr) * mu_ref[...]
        r = xm[:, :RW_WIDTH]
        kx = xm[:, RW_WIDTH:2 * RW_WIDTH]
        vx = xm[:, 2 * RW_WIDTH:RW_OFF_W]
        xwa = xm[:, RW_OFF_W:RW_OFF_G]
        dg = xm[:, RW_OFF_G:]
        th, tl = _split2(jnp.tanh(xwa))
        zw = w0_ref[...] + (_dot(th, w2h_ref[...]) + _dot(th, w2l_ref[...]) + _dot(tl, w2h_ref[...]))
        za = a0_ref[...] + _dot(xwa.astype(BF16), a2_ref[...])
        g = _dot(_sigmoid(dg).astype(BF16), g2_ref[...])
        lw = -math.exp(-0.5) * _sigmoid(zw)
        asig = _sigmoid(za)
        kkv = kx * kk_ref[...]
        kkn = kkv / jnp.maximum(jnp.sqrt(_seg_sum(kkv * kkv, e_ref)), 1e-12)
        kmod = kx * (1.0 + (asig - 1.0) * ka_ref[...])
        a_ = -kkn
        b_ = kkn * asig
        k_ = kmod
        if t_valid < cc:
            valid = row1 < t_valid
            lw = jnp.where(valid, lw, 0.0)
            b_ = jnp.where(valid, b_, 0.0)
            k_ = jnp.where(valid, k_, 0.0)
        hi, mid, lo = _split3(lw)
        cum = _dot(tri, hi) + _dot(tri, mid) + _dot(tri, lo)
        cmid = cum[cc // 2 - 1:cc // 2, :]
        e_in = jnp.exp(cum - cmid)
        e_ex = jnp.exp(cum - lw - cmid)
        e_neg = jnp.exp(cmid - cum)
        g_mid = jnp.exp(cmid)
        g_end = jnp.exp(cum[cc - 1:cc, :] - cmid)
        at = a_ * e_ex
        rt = r * e_in
        bt = b_ * e_neg
        kt = k_ * e_neg

        ys = []
        for p in range(RW_HEADS // 2):
            lanes = slice(p * LANE, (p + 1) * LANE)
            a_s = _stack_pair(at[:, lanes]).astype(BF16)
            r_s = _stack_pair(rt[:, lanes]).astype(BF16)
            b_s = _stack_pair(bt[:, lanes]).astype(BF16)
            k_s = _stack_pair(kt[:, lanes]).astype(BF16)
            v_s = _stack_pair(vx[:, lanes]).astype(BF16)
            mfull = _dot_nt(jnp.concatenate([a_s, r_s], axis=0), jnp.concatenate([b_s, k_s], axis=0))
            n2 = 2 * cc
            lmat = jnp.where(strict, mfull[:n2, :n2], 0.0)
            mak = jnp.where(strict, mfull[:n2, n2:], 0.0).astype(BF16)
            mrb = jnp.where(incl, mfull[n2:, :n2], 0.0).astype(BF16)
            mrk = jnp.where(incl, mfull[n2:, n2:], 0.0).astype(BF16)
            sp = s_ref[p] * g_mid[:, lanes]
            spb = sp.astype(BF16)
            x = _dot_nt(a_s, spb) + _dot(mak, v_s)
            steps = int(math.log2(cc))
            for s_i in range(steps):
                lb = lmat.astype(BF16)
                x = x + _dot(lb, x.astype(BF16))
                if s_i + 1 < steps:
                    lmat = _dot(lb, lb)
            ub = x.astype(BF16)
            y2 = _dot_nt(r_s, spb) + _dot(mrb, ub) + _dot(mrk, v_s)
            s_ref[p] = (sp + _dot_tn(ub, b_s) + _dot_tn(v_s, k_s)) * g_end[:, lanes]
            ys.append(y2[:cc, :] + y2[cc:, :])
        y = jnp.concatenate(ys, axis=1)

        ym = _seg_sum(y, e_ref) * (1.0 / HEAD_DIM)
        yc = y - ym
        yv = _seg_sum(yc * yc, e_ref) * (1.0 / HEAD_DIM)
        yn = yc * lax.rsqrt(yv + GN_EPS) * lng_ref[...] + lnb_ref[...]
        bonus = _seg_sum(r * kmod * rk_ref[...], e_ref) * vx
        y_ref[0, pl.ds(r0, cc), :] = ((yn + bonus) * g).astype(BF16)
        return 0

    lax.fori_loop(0, n_chunks, chunk, 0)

    @pl.when(t == pl.num_programs(1) - 1)
    def _():
        sfin_ref[0] = s_ref[...]


def _rwkv(pr, prev, s0, w, t_valid):
    batch, seq, _ = pr.shape
    rows = min(RW_ROWS, seq)
    n_pairs = RW_HEADS // 2
    vec = lambda n: _const_spec((1, n))
    return pl.pallas_call(
        functools.partial(_rwkv_kernel, n_chunks=rows // RW_CHUNK, t_valid=t_valid),
        grid=(batch, seq // rows),
        in_specs=[pl.BlockSpec((1, rows, RW_COLS), lambda b, t: (b, t, 0)),
                  pl.BlockSpec((1, 1, RW_COLS), lambda b, t: (b, 0, 0)),
                  pl.BlockSpec((1, n_pairs, LANE, LANE), lambda b, t: (b, 0, 0, 0)),
                  vec(RW_COLS), vec(RW_WIDTH), _const_spec((LANE, RW_WIDTH)), _const_spec((LANE, RW_WIDTH)),
                  vec(RW_WIDTH), _const_spec((LANE, RW_WIDTH)), _const_spec((LANE, RW_WIDTH)),
                  vec(RW_WIDTH), vec(RW_WIDTH), vec(RW_WIDTH), vec(RW_WIDTH), vec(RW_WIDTH),
                  _const_spec((RW_WIDTH, RW_WIDTH))],
        out_specs=[pl.BlockSpec((1, rows, RW_WIDTH), lambda b, t: (b, t, 0)),
                   pl.BlockSpec((1, n_pairs, LANE, LANE), lambda b, t: (b, 0, 0, 0))],
        out_shape=[jax.ShapeDtypeStruct((batch, seq, RW_WIDTH), BF16),
                   jax.ShapeDtypeStruct((batch, n_pairs, LANE, LANE), F32)],
        scratch_shapes=[pltpu.VMEM((n_pairs, LANE, LANE), F32), pltpu.VMEM((1, RW_COLS), F32)],
        compiler_params=_params(("parallel", "arbitrary")),
        name="rwkv7",
    )(pr, prev, s0, w["mu"], w["w0"], w["w2h"], w["w2l"], w["a0"], w["a2"], w["g2"],
      w["kk"], w["ka"], w["rk"], w["ln_g"], w["ln_b"], w["seg"])


def _even_out_kernel(x_ref, of_ref, yr_ref, wo_ref, g3_ref, gpre_ref, gpost_ref, win_ref, wout_ref,
                     o_ref, act_ref):
    mixed = _dot(of_ref[...], wo_ref[:FOX_WIDTH, :]) + _dot(yr_ref[...], wo_ref[FOX_WIDTH:, :])
    x = x_ref[...] + _rms(mixed, g3_ref[...])
    o_ref[...] = _ffn_apply(x, gpre_ref[...], gpost_ref[...], win_ref, wout_ref, act_ref)


def _even_out(x, of, yr, wo, g3, gpre, gpost, win, wout):
    n = x.shape[0]
    tm = min(ROW_TILE, n)
    vec = _const_spec((1, D_MODEL))
    return pl.pallas_call(
        _even_out_kernel,
        grid=(n // tm,),
        in_specs=[_row_spec(tm, D_MODEL), _row_spec(tm, FOX_WIDTH), _row_spec(tm, RW_WIDTH),
                  _const_spec((D_MODEL, D_MODEL)), vec, vec, vec,
                  _const_spec((D_MODEL, 2 * D_FF)), _const_spec((D_FF, D_MODEL))],
        out_specs=_row_spec(tm, D_MODEL),
        out_shape=jax.ShapeDtypeStruct((n, D_MODEL), F32),
        scratch_shapes=[pltpu.VMEM((tm, D_FF), BF16)],
        compiler_params=_params(("parallel",)),
        name="even_out_ffn",
    )(x, of, yr, wo, g3, gpre, gpost, win, wout)


def _gelu(x):
    return 0.5 * x * (1.0 + lax.erf(x * (2.0 ** -0.5)))


def _odd_kernel(x_ref, g2_ref, win_ref, lng_ref, lnb_ref, wm_ref, bs_ref, wo_ref, g3_ref,
                gpre_ref, gpost_ref, fwin_ref, fwout_ref, o_ref, *rest):
    gv_ref = rest[0] if len(rest) == 3 else None
    act_ref, gate_ref = rest[-2:]
    x = x_ref[...]
    tm = x.shape[0]
    h = _rms(x, g2_ref[...]).astype(BF16)
    v = _gelu(_dot(h, win_ref[:, GM_WIDTH:]))
    vm = jnp.mean(v, axis=-1, keepdims=True)
    vc = v - vm
    var = jnp.mean(vc * vc, axis=-1, keepdims=True)
    v = vc * lax.rsqrt(var + LN_EPS) * lng_ref[...] + lnb_ref[...]
    if gv_ref is not None:
        gv_ref[...] = v
    vb = v.astype(BF16)
    u = _gelu(_dot(h, win_ref[:, :GM_WIDTH]))
    for c in range(tm // GM_CHUNK):
        rows = slice(c * GM_CHUNK, (c + 1) * GM_CHUNK)
        sp = jnp.concatenate(
            [_dot(wm_ref[g], vb[rows, g * LANE:(g + 1) * LANE]) for g in range(GM_GROUPS)], axis=1)
        gate_ref[rows, :] = (u[rows, :] * (sp + bs_ref[...])).astype(BF16)
    mixed = _dot(gate_ref[...], wo_ref[...])
    x = x + _rms(mixed, g3_ref[...])
    o_ref[...] = _ffn_apply(x, gpre_ref[...], gpost_ref[...], fwin_ref, fwout_ref, act_ref)


def _odd(x, g2, win, lng, lnb, wm, bs, wo, g3, gpre, gpost, fwin, fwout, emit_v):
    n = x.shape[0]
    tm = min(ROW_TILE, n)
    vec = _const_spec((1, D_MODEL))
    return pl.pallas_call(
        _odd_kernel,
        grid=(n // tm,),
        in_specs=[_row_spec(tm, D_MODEL), vec, _const_spec((D_MODEL, 2 * GM_WIDTH)), vec, vec,
                  _const_spec((GM_GROUPS, GM_CHUNK, GM_CHUNK)), _const_spec((GM_CHUNK, GM_WIDTH)),
                  _const_spec((GM_WIDTH, D_MODEL)), vec, vec, vec,
                  _const_spec((D_MODEL, 2 * D_FF)), _const_spec((D_FF, D_MODEL))],
        out_specs=[_row_spec(tm, D_MODEL)] + [_row_spec(tm, GM_WIDTH)] * emit_v,
        out_shape=[jax.ShapeDtypeStruct((n, D_MODEL), F32)] + [jax.ShapeDtypeStruct((n, GM_WIDTH), F32)] * emit_v,
        scratch_shapes=[pltpu.VMEM((tm, D_FF), BF16), pltpu.VMEM((tm, GM_WIDTH), BF16)],
        compiler_params=_params(("parallel",)),
        name="odd_mixer_ffn",
    )(x, g2, win, lng, lnb, wm, bs, wo, g3, gpre, gpost, fwin, fwout)


def _pair_block_diag(s):
    b = s.shape[0]
    s = s.reshape(b, RW_HEADS // 2, 2, HEAD_DIM, HEAD_DIM)
    z = jnp.zeros_like(s[:, :, 0])
    top = jnp.concatenate([s[:, :, 0], z], axis=-1)
    bot = jnp.concatenate([z, s[:, :, 1]], axis=-1)
    return jnp.concatenate([top, bot], axis=-2)


def _pair_unblock(s):
    b = s.shape[0]
    h0 = s[:, :, :HEAD_DIM, :HEAD_DIM]
    h1 = s[:, :, HEAD_DIM:, HEAD_DIM:]
    return jnp.stack([h0, h1], axis=2).reshape(b, RW_HEADS, HEAD_DIM, HEAD_DIM)


def kernel(x_prompt, x_sample, cache_fox_k, cache_fox_v, cache_fox_logf, state_rwkv, state_rwkv_shift,
           norm_g, ffn_w_in, ffn_w_out, even_w_in, fox_bf, rw_mu, rw_w0, rw_w2, rw_a0, rw_a2, rw_g2,
           rw_kk, rw_ka, rw_rk, rw_ln_g, rw_ln_b, even_w_out, gm_w_in, gm_ln_g, gm_ln_b, gm_w_s,
           gm_b_s, gm_w_out):
    batch, seq, _ = x_prompt.shape
    dbatch, dseq, _ = x_sample.shape
    past = cache_fox_k.shape[2]
    depth = norm_g.shape[0]
    xp = x_prompt.reshape(batch * seq, D_MODEL)
    xs = x_sample.reshape(dbatch * dseq, D_MODEL)
    ng = lambda layer, i: norm_g[layer, i].reshape(1, D_MODEL)
    win = ffn_w_in.astype(BF16)
    wout = ffn_w_out.astype(BF16)

    fk_p, fv_p, fl_p, rs_p, rsh_p = [], [], [], [], []
    fk_s, fv_s, fl_s, rs_s, rsh_s = [], [], [], [], []
    gv_s = []
    for layer in range(depth):
        j = layer // 2
        if layer % 2 == 0:
            xp = _ffn(xp, ng(layer, 0), ng(layer, 1), win[layer, 0], wout[layer, 0])
            xs = _ffn(xs, ng(layer, 0), ng(layer, 1), win[layer, 0], wout[layer, 0])
            ew = even_w_in[j]
            wqkv = ew[:, :3 * FOX_WIDTH].astype(BF16)
            wf = jnp.pad(ew[:, 3 * FOX_WIDTH:FOX_COLS], ((0, 0), (0, LANE - FOX_HEADS))).astype(BF16)
            bf = jnp.pad(fox_bf[j], (0, LANE - FOX_HEADS)).reshape(1, LANE)
            wrw = ew[:, FOX_COLS:].astype(BF16)
            w2 = jnp.pad(rw_w2[j], ((0, LANE - rw_w2.shape[1]), (0, 0)))
            w2h = w2.astype(BF16)
            head_of = jnp.arange(RW_WIDTH) // HEAD_DIM
            rw = dict(
                mu=rw_mu[j].reshape(1, RW_COLS), w0=rw_w0[j].reshape(1, RW_WIDTH),
                w2h=w2h, w2l=(w2 - w2h.astype(F32)).astype(BF16),
                a0=rw_a0[j].reshape(1, RW_WIDTH),
                a2=jnp.pad(rw_a2[j], ((LANE - rw_a2.shape[1], 0), (0, 0))).astype(BF16),
                g2=rw_g2[j].astype(BF16),
                kk=rw_kk[j].reshape(1, RW_WIDTH), ka=rw_ka[j].reshape(1, RW_WIDTH),
                rk=rw_rk[j].reshape(1, RW_WIDTH),
                ln_g=rw_ln_g[j].reshape(1, RW_WIDTH), ln_b=rw_ln_b[j].reshape(1, RW_WIDTH),
                seg=(head_of[:, None] == head_of[None, :]).astype(BF16))
            wo = even_w_out[j].astype(BF16)

            q, k, v, lf, lft, pr = _proj(xp, ng(layer, 2), wqkv, wf, bf, wrw)
            of = _fox_prompt(q, k, v, lft, batch, seq)
            pr3 = pr.reshape(batch, seq, RW_COLS)
            yr, sfin = _rwkv(pr3, jnp.zeros((batch, 1, RW_COLS), F32),
                             jnp.zeros((batch, RW_HEADS // 2, LANE, LANE), F32), rw, RW_CHUNK)
            xp = _even_out(xp, of, yr.reshape(batch * seq, RW_WIDTH), wo, ng(layer, 3),
                           ng(layer, 4), ng(layer, 5), win[layer, 1], wout[layer, 1])
            fk_p.append(k.reshape(batch, seq, FOX_HEADS, HEAD_DIM))
            fv_p.append(v.reshape(batch, seq, FOX_HEADS, HEAD_DIM))
            fl_p.append(lf.reshape(batch, seq, FOX_HEADS))
            rs_p.append(_pair_unblock(sfin))
            rsh_p.append(pr3[:, seq - 1:, :])

            q, k, v, lf, lft, pr = _proj(xs, ng(layer, 2), wqkv, wf, bf, wrw)
            lfn_t = jnp.swapaxes(lf.reshape(dbatch, dseq, FOX_HEADS), 1, 2)
            lfc_t = jnp.swapaxes(cache_fox_logf[j], 1, 2)
            of = _fox_sample(q, k, v, lfn_t, cache_fox_k[j].reshape(dbatch, past, FOX_WIDTH),
                             cache_fox_v[j].reshape(dbatch, past, FOX_WIDTH), lfc_t, dbatch, dseq, past)
            pr3 = pr.reshape(dbatch, dseq, RW_COLS)
            pad = (-dseq) % RW_CHUNK
            yr, sfin = _rwkv(jnp.pad(pr3, ((0, 0), (0, pad), (0, 0))), state_rwkv_shift[j],
                             _pair_block_diag(state_rwkv[j]), rw, dseq if pad else RW_CHUNK)
            xs = _even_out(xs, of, yr[:, :dseq].reshape(dbatch * dseq, RW_WIDTH), wo, ng(layer, 3),
                           ng(layer, 4), ng(layer, 5), win[layer, 1], wout[layer, 1])
            fk_s.append(k.reshape(dbatch, dseq, FOX_HEADS, HEAD_DIM))
            fv_s.append(v.reshape(dbatch, dseq, FOX_HEADS, HEAD_DIM))
            fl_s.append(lf.reshape(dbatch, dseq, FOX_HEADS))
            rs_s.append(_pair_unblock(sfin))
            rsh_s.append(pr3[:, dseq - 1:, :])
        else:
            xp = _ffn(xp, ng(layer, 0), ng(layer, 1), win[layer, 0], wout[layer, 0])
            xs = _ffn(xs, ng(layer, 0), ng(layer, 1), win[layer, 0], wout[layer, 0])
            cpos = jnp.arange(GM_CHUNK) // CHUNK
            wm = jnp.where(cpos[None, :] <= cpos[:, None], gm_w_s[j], 0.0)
            bs = jnp.repeat(gm_b_s[j].T, GM_WIDTH // GM_GROUPS, axis=1)
            reps = GM_CHUNK // dseq
            wm_s = jnp.einsum('ab,gts->gatbs', jnp.eye(reps, dtype=F32), wm[:, :dseq, :dseq])
            wm_s = wm_s.reshape(GM_GROUPS, GM_CHUNK, GM_CHUNK)
            bs_s = jnp.tile(bs[:dseq], (reps, 1))
            odd = (ng(layer, 2), gm_w_in[j].astype(BF16), gm_ln_g[j].reshape(1, GM_WIDTH),
                   gm_ln_b[j].reshape(1, GM_WIDTH))
            tail = (gm_w_out[j].astype(BF16), ng(layer, 3), ng(layer, 4), ng(layer, 5),
                    win[layer, 1], wout[layer, 1])
            xp, = _odd(xp, *odd, wm.astype(BF16), bs, *tail, emit_v=False)
            xs, gv = _odd(xs, *odd, wm_s.astype(BF16), bs_s, *tail, emit_v=True)
            gv_s.append(gv.reshape(dbatch, dseq, GM_WIDTH))
    return (xp.reshape(batch, seq, D_MODEL), xs.reshape(dbatch, dseq, D_MODEL),
            jnp.stack(fk_p), jnp.stack(fv_p), jnp.stack(fl_p), jnp.stack(rs_p), jnp.stack(rsh_p),
            jnp.stack(fk_s), jnp.stack(fv_s), jnp.stack(fl_s), jnp.stack(rs_s), jnp.stack(rsh_s),
            jnp.stack(gv_s))
```

```python
import functools
import math

import jax
import jax.numpy as jnp
from jax import lax
from jax.experimental import pallas as pl
from jax.experimental.pallas import tpu as pltpu

F32 = jnp.float32
BF16 = jnp.bfloat16

D_MODEL = 1024
D_FF = 2816
HEAD_DIM = 64
FOX_WIDTH = 512
FOX_HEADS = 8
RW_WIDTH = 512
RW_HEADS = 8
RW_COLS = 1792
RW_OFF_W = 1536
RW_OFF_G = 1664
FOX_COLS = 3 * FOX_WIDTH + FOX_HEADS
GM_WIDTH = 1024
GM_CHUNK = 128
GM_GROUPS = 8
CHUNK = 64
FFN_RES_SCALE = 0.5
RMS_EPS = 1e-6
LN_EPS = 1e-5
GN_EPS = 64e-5

LANE = 128
ROW_TILE = 512
FF_CHUNK = 256
FOX_BLOCK = 256
RW_CHUNK = 64
RW_ROWS = 256
NEG_BIG = -1e30
VMEM_LIMIT = 56 * 1024 * 1024


def _params(sem):
    return pltpu.CompilerParams(dimension_semantics=sem, vmem_limit_bytes=VMEM_LIMIT)


def _const_spec(shape):
    n = len(shape)
    return pl.BlockSpec(shape, lambda *_: (0,) * n, pipeline_mode=pl.Buffered(1))


def _dot(a, b):
    return jnp.dot(a, b, preferred_element_type=F32)


def _dot_nt(a, b):
    return lax.dot_general(a, b, (((1,), (1,)), ((), ())), preferred_element_type=F32)


def _dot_tn(a, b):
    return lax.dot_general(a, b, (((0,), (0,)), ((), ())), preferred_element_type=F32)


def _split2(x):
    hi = x.astype(BF16)
    lo = (x - hi.astype(F32)).astype(BF16)
    return hi, lo


def _split3(x):
    hi = x.astype(BF16)
    r1 = x - hi.astype(F32)
    mid = r1.astype(BF16)
    lo = (r1 - mid.astype(F32)).astype(BF16)
    return hi, mid, lo


def _rms(x, g):
    return x * lax.rsqrt(jnp.mean(x * x, axis=-1, keepdims=True) + RMS_EPS) * g


def _sigmoid(x):
    return 1.0 / (1.0 + jnp.exp(-x))


def _log_sigmoid(z):
    return jnp.minimum(z, 0.0) - jnp.log1p(jnp.exp(-jnp.abs(z)))


def _ffn_apply(x, gpre, gpost, win_ref, wout_ref, act_ref):
    h = _rms(x, gpre).astype(BF16)
    for c in range(D_FF // FF_CHUNK):
        lo = c * FF_CHUNK
        gate = _dot(h, win_ref[:, lo:lo + FF_CHUNK])
        up = _dot(h, win_ref[:, D_FF + lo:D_FF + lo + FF_CHUNK])
        act_ref[:, lo:lo + FF_CHUNK] = (gate * _sigmoid(gate) * up).astype(BF16)
    y = _dot(act_ref[...], wout_ref[...])
    return x + FFN_RES_SCALE * _rms(y, gpost)


def _ffn_kernel(x_ref, gpre_ref, gpost_ref, win_ref, wout_ref, o_ref, act_ref):
    o_ref[...] = _ffn_apply(x_ref[...], gpre_ref[...], gpost_ref[...], win_ref, wout_ref, act_ref)


def _row_spec(tm, width):
    return pl.BlockSpec((tm, width), lambda i: (i, 0))


def _ffn(x, gpre, gpost, win, wout):
    n = x.shape[0]
    tm = min(ROW_TILE, n)
    return pl.pallas_call(
        _ffn_kernel,
        grid=(n // tm,),
        in_specs=[_row_spec(tm, D_MODEL), _const_spec((1, D_MODEL)), _const_spec((1, D_MODEL)),
                  _const_spec((D_MODEL, 2 * D_FF)), _const_spec((D_FF, D_MODEL))],
        out_specs=_row_spec(tm, D_MODEL),
        out_shape=jax.ShapeDtypeStruct((n, D_MODEL), F32),
        scratch_shapes=[pltpu.VMEM((tm, D_FF), BF16)],
        compiler_params=_params(("parallel",)),
        name="ffn",
    )(x, gpre, gpost, win, wout)


def _proj_kernel(x_ref, g_ref, wqkv_ref, wf_ref, bf_ref, wrw_ref,
                 q_ref, k_ref, v_ref, lf_ref, lfp_ref, pr_ref):
    h = _rms(x_ref[...], g_ref[...]).astype(BF16)
    q_ref[...] = _dot(h, wqkv_ref[:, :FOX_WIDTH]).astype(BF16)
    k_ref[...] = _dot(h, wqkv_ref[:, FOX_WIDTH:2 * FOX_WIDTH])
    v_ref[...] = _dot(h, wqkv_ref[:, 2 * FOX_WIDTH:])
    lf = _log_sigmoid(_dot(h, wf_ref[...]) + bf_ref[...])
    lf_ref[...] = lf[:, :FOX_HEADS]
    lfp_ref[...] = lf
    pr_ref[...] = _dot(h, wrw_ref[...])


def _proj(x, g, wqkv, wf, bf, wrw):
    n = x.shape[0]
    tm = min(ROW_TILE, n)
    return pl.pallas_call(
        _proj_kernel,
        grid=(n // tm,),
        in_specs=[_row_spec(tm, D_MODEL), _const_spec((1, D_MODEL)),
                  _const_spec((D_MODEL, 3 * FOX_WIDTH)), _const_spec((D_MODEL, LANE)),
                  _const_spec((1, LANE)), _const_spec((D_MODEL, RW_COLS))],
        out_specs=[_row_spec(tm, FOX_WIDTH), _row_spec(tm, FOX_WIDTH), _row_spec(tm, FOX_WIDTH),
                   _row_spec(tm, FOX_HEADS), _row_spec(tm, LANE), _row_spec(tm, RW_COLS)],
        out_shape=[jax.ShapeDtypeStruct((n, FOX_WIDTH), BF16),
                   jax.ShapeDtypeStruct((n, FOX_WIDTH), F32),
                   jax.ShapeDtypeStruct((n, FOX_WIDTH), F32),
                   jax.ShapeDtypeStruct((n, FOX_HEADS), F32),
                   jax.ShapeDtypeStruct((n, LANE), F32),
                   jax.ShapeDtypeStruct((n, RW_COLS), F32)],
        compiler_params=_params(("parallel",)),
        name="even_proj",
    )(x, g, wqkv, wf, bf, wrw)


def _lane_cumsum(rows, carry):
    r = lax.broadcasted_iota(jnp.int32, (LANE, LANE), 0)
    c = lax.broadcasted_iota(jnp.int32, (LANE, LANE), 1)
    upper = (r <= c).astype(BF16)
    hi, mid, lo = _split3(rows)
    return _dot(hi, upper) + _dot(mid, upper) + _dot(lo, upper) + carry


def _fox_prompt_kernel(q_ref, k_ref, v_ref, lfp_ref, o_ref, kx_ref, vt_ref, qx_ref, m_ref, l_ref, acc_ref,
                       *, seq):
    i = pl.program_id(1)
    bq = FOX_BLOCK
    n_pairs = FOX_HEADS // 2

    @pl.when(i == 0)
    def _():
        r = lax.broadcasted_iota(jnp.int32, (LANE, LANE), 0)
        c = lax.broadcasted_iota(jnp.int32, (LANE, LANE), 1)
        tri = (c <= r).astype(BF16)
        carry = jnp.zeros((1, LANE), F32)
        for blk in range(seq // LANE):
            rows = slice(blk * LANE, (blk + 1) * LANE)
            hi, mid, lo = _split3(jnp.where(c < FOX_HEADS, lfp_ref[rows, :], 0.0))
            cum = _dot(tri, hi) + _dot(tri, mid) + _dot(tri, lo) + carry
            carry = cum[LANE - 1:LANE, :]
            chi, cmid, clo = _split3(cum)
            feat = (chi.astype(F32) + pltpu.roll(cmid.astype(F32), FOX_HEADS, 1)
                    + pltpu.roll(clo.astype(F32), 2 * FOX_HEADS, 1)).astype(BF16)
            for p in range(n_pairs):
                lanes = slice(p * LANE, (p + 1) * LANE)
                kx_ref[p, rows, :LANE] = k_ref[rows, lanes].astype(BF16)
                kx_ref[p, rows, LANE:] = feat
                vt_ref[lanes, rows] = v_ref[rows, lanes].T.astype(BF16)

    q0 = pl.multiple_of(i * bq, bq)
    lane = lax.broadcasted_iota(jnp.int32, (1, LANE), 1)
    krow = lax.broadcasted_iota(jnp.int32, (bq, bq), 0)
    qcol = lax.broadcasted_iota(jnp.int32, (bq, bq), 1)
    causal = krow <= qcol
    for p in range(n_pairs):
        qp = q_ref[:, p * LANE:(p + 1) * LANE].astype(F32) * (HEAD_DIM ** -0.5)
        for hh in range(2):
            h = 2 * p + hh
            in_head = (lane >= HEAD_DIM) if hh else (lane < HEAD_DIM)
            pick = (lane == h) | (lane == h + FOX_HEADS) | (lane == h + 2 * FOX_HEADS)
            sel = jnp.where(pick, -1.0, 0.0).astype(BF16)
            qx_ref[h, :, :LANE] = jnp.where(in_head, qp, 0.0).astype(BF16)
            qx_ref[h, :, LANE:] = jnp.broadcast_to(sel, (bq, LANE))

    heads = range(FOX_HEADS)
    head_rows = [slice(h * HEAD_DIM, (h + 1) * HEAD_DIM) for h in heads]

    def tile(j0, diag):
        sts = [_dot_nt(kx_ref[h // 2, pl.ds(j0, bq), :], qx_ref[h]) for h in heads]
        pts, alphas = [], []
        for h in heads:
            st = sts[h]
            if diag:
                st = jnp.where(causal, st, NEG_BIG)
                m_new = jnp.max(st, axis=0, keepdims=True)
                pt = jnp.exp(st - m_new)
                l_ref[h:h + 1, :] = jnp.sum(pt, axis=0, keepdims=True)
            else:
                m_old = m_ref[h:h + 1, :]
                m_new = jnp.maximum(m_old, jnp.max(st, axis=0, keepdims=True))
                alpha = jnp.exp(m_old - m_new)
                pt = jnp.exp(st - m_new)
                l_ref[h:h + 1, :] = alpha * l_ref[h:h + 1, :] + jnp.sum(pt, axis=0, keepdims=True)
                alphas.append(alpha)
            m_ref[h:h + 1, :] = m_new
            pts.append(pt.astype(BF16))
        pvs = [_dot(vt_ref[head_rows[h], pl.ds(j0, bq)], pts[h]) for h in heads]
        for h in heads:
            acc_ref[head_rows[h], :] = pvs[h] if diag else alphas[h] * acc_ref[head_rows[h], :] + pvs[h]

    tile(q0, True)

    def body(j, _):
        tile(pl.multiple_of(j * bq, bq), False)
        return 0

    lax.fori_loop(0, i, body, 0)
    inv_l = jnp.concatenate([jnp.broadcast_to(1.0 / l_ref[h:h + 1, :], (HEAD_DIM, bq)) for h in heads], axis=0)
    o_ref[...] = (acc_ref[...] * inv_l).T.astype(BF16)


def _fox_prompt(q, k, v, lfp, batch, seq):
    nq = seq // FOX_BLOCK
    return pl.pallas_call(
        functools.partial(_fox_prompt_kernel, seq=seq),
        grid=(batch, nq),
        in_specs=[pl.BlockSpec((FOX_BLOCK, FOX_WIDTH), lambda b, i: (b * nq + i, 0)),
                  pl.BlockSpec((seq, FOX_WIDTH), lambda b, i: (b, 0)),
                  pl.BlockSpec((seq, FOX_WIDTH), lambda b, i: (b, 0)),
                  pl.BlockSpec((seq, LANE), lambda b, i: (b, 0))],
        out_specs=pl.BlockSpec((FOX_BLOCK, FOX_WIDTH), lambda b, i: (b * nq + i, 0)),
        out_shape=jax.ShapeDtypeStruct((batch * seq, FOX_WIDTH), BF16),
        scratch_shapes=[pltpu.VMEM((FOX_HEADS // 2, seq, 2 * LANE), BF16),
                        pltpu.VMEM((FOX_WIDTH, seq), BF16),
                        pltpu.VMEM((FOX_HEADS, FOX_BLOCK, 2 * LANE), BF16),
                        pltpu.VMEM((FOX_HEADS, FOX_BLOCK), F32),
                        pltpu.VMEM((FOX_HEADS, FOX_BLOCK), F32),
                        pltpu.VMEM((FOX_WIDTH, FOX_BLOCK), F32)],
        compiler_params=_params(("parallel", "arbitrary")),
        name="fox_prompt",
    )(q, k, v, lfp)


def _fox_sample_kernel(q_ref, kn_ref, vn_ref, lfn_ref, kc_ref, vc_ref, lfc_ref, o_ref,
                       kb_ref, vb_ref, s_ref, *, past, tnew):
    rows = FOX_HEADS * tnew
    q = q_ref[...].astype(F32) * (HEAD_DIM ** -0.5)
    lane_head = lax.broadcasted_iota(jnp.int32, (tnew, FOX_WIDTH), 1) // HEAD_DIM
    qs = jnp.concatenate([jnp.where(lane_head == h, q, 0.0) for h in range(FOX_HEADS)], axis=0).astype(BF16)

    blk = 512
    def cast(c, _):
        r0 = pl.multiple_of(c * blk, blk)
        kb_ref[pl.ds(r0, blk), :] = kc_ref[0, pl.ds(r0, blk), :].astype(BF16)
        vb_ref[pl.ds(r0, blk), :] = vc_ref[0, pl.ds(r0, blk), :].astype(BF16)
        return 0
    lax.fori_loop(0, past // blk, cast, 0)

    carry = jnp.zeros((FOX_HEADS, 1), F32)
    cts = []
    for b in range(past // LANE):
        c = _lane_cumsum(lfc_ref[0, :, b * LANE:(b + 1) * LANE], carry)
        cts.append(c)
        carry = c[:, LANE - 1:LANE]
    total = carry
    r = lax.broadcasted_iota(jnp.int32, (tnew, tnew), 0)
    c = lax.broadcasted_iota(jnp.int32, (tnew, tnew), 1)
    hi, mid, lo = _split3(lfn_ref[0])
    upper = (r <= c).astype(BF16)
    cnew = _dot(hi, upper) + _dot(mid, upper) + _dot(lo, upper)

    for b in range(past // LANE):
        sl = slice(b * LANE, (b + 1) * LANE)
        s = _dot_nt(qs, kb_ref[sl, :])
        ck = cts[b] - total
        bias = jnp.concatenate([jnp.broadcast_to(ck[h:h + 1, :], (tnew, LANE)) for h in range(FOX_HEADS)], axis=0)
        s_ref[:, sl] = s - bias
    sn = _dot_nt(qs, kn_ref[...].astype(BF16))
    bias_n = jnp.concatenate([jnp.broadcast_to(cnew[h:h + 1, :], (tnew, tnew)) for h in range(FOX_HEADS)], axis=0)
    tq = lax.broadcasted_iota(jnp.int32, (rows, tnew), 0) % tnew
    tk = lax.broadcasted_iota(jnp.int32, (rows, tnew), 1)
    sn = jnp.where(tk <= tq, sn - bias_n, NEG_BIG)

    sc = s_ref[...]
    m = jnp.maximum(jnp.max(sc, axis=1, keepdims=True), jnp.max(sn, axis=1, keepdims=True))
    pc = jnp.exp(sc - m)
    pn = jnp.exp(sn - m)
    l = jnp.sum(pc, axis=1, keepdims=True) + jnp.sum(pn, axis=1, keepdims=True)
    o = (_dot(pc.astype(BF16), vb_ref[...]) + _dot(pn.astype(BF16), vn_ref[...].astype(BF16))) / l
    out = jnp.zeros((tnew, FOX_WIDTH), F32)
    for h in range(FOX_HEADS):
        out = out + jnp.where(lane_head == h, o[h * tnew:(h + 1) * tnew, :], 0.0)
    o_ref[...] = out.astype(BF16)


def _fox_sample(q, kn, vn, lfn_t, kc, vc, lfc_t, batch, tnew, past):
    return pl.pallas_call(
        functools.partial(_fox_sample_kernel, past=past, tnew=tnew),
        grid=(batch,),
        in_specs=[pl.BlockSpec((tnew, FOX_WIDTH), lambda b: (b, 0)),
                  pl.BlockSpec((tnew, FOX_WIDTH), lambda b: (b, 0)),
                  pl.BlockSpec((tnew, FOX_WIDTH), lambda b: (b, 0)),
                  pl.BlockSpec((1, FOX_HEADS, tnew), lambda b: (b, 0, 0)),
                  pl.BlockSpec((1, past, FOX_WIDTH), lambda b: (b, 0, 0)),
                  pl.BlockSpec((1, past, FOX_WIDTH), lambda b: (b, 0, 0)),
                  pl.BlockSpec((1, FOX_HEADS, past), lambda b: (b, 0, 0))],
        out_specs=pl.BlockSpec((tnew, FOX_WIDTH), lambda b: (b, 0)),
        out_shape=jax.ShapeDtypeStruct((batch * tnew, FOX_WIDTH), BF16),
        scratch_shapes=[pltpu.VMEM((past, FOX_WIDTH), BF16), pltpu.VMEM((past, FOX_WIDTH), BF16),
                        pltpu.VMEM((FOX_HEADS * tnew, past), F32)],
        compiler_params=_params(("parallel",)),
        name="fox_sample",
    )(q, kn, vn, lfn_t, kc, vc, lfc_t)


def _seg_sum(z, e_ref):
    zb = z.astype(BF16)
    return jnp.concatenate(
        [_dot(zb[:, p * LANE:(p + 1) * LANE], e_ref[...]) for p in range(z.shape[1] // LANE)], axis=1)


def _rwkv_kernel(pr_ref, prev_ref, s0_ref, mu_ref, w0_ref, w2h_ref, w2l_ref, a0_ref, a2_ref, g2_ref,
                 kk_ref, ka_ref, rk_ref, lng_ref, lnb_ref, e_ref,
                 y_ref, sfin_ref, s_ref, pv_ref, *, n_chunks, t_valid):
    cc = RW_CHUNK
    t = pl.program_id(1)

    @pl.when(t == 0)
    def _():
        s_ref[...] = s0_ref[0]
        pv_ref[...] = prev_ref[0]

    rows = n_chunks * cc
    ri = lax.broadcasted_iota(jnp.int32, (rows, rows), 0)
    ci = lax.broadcasted_iota(jnp.int32, (rows, rows), 1)
    tri = ((ci <= ri) & (ci // cc == ri // cc)).astype(BF16)
    r2 = lax.broadcasted_iota(jnp.int32, (2 * cc, 2 * cc), 0) % cc
    c2 = lax.broadcasted_iota(jnp.int32, (2 * cc, 2 * cc), 1) % cc
    strict = c2 < r2
    incl = c2 <= r2
    row1 = lax.broadcasted_iota(jnp.int32, (rows, 1), 0)

    pr = pr_ref[0]
    prev_seq = jnp.where(row1 == 0, pv_ref[...], pltpu.roll(pr, 1, 0))
    pv_ref[...] = pr[rows - 1:rows, :]
    xm = pr + (prev_seq - pr) * mu_ref[...]
    r = xm[:, :RW_WIDTH]
    kx = xm[:, RW_WIDTH:2 * RW_WIDTH]
    vx = xm[:, 2 * RW_WIDTH:RW_OFF_W]
    xwa = xm[:, RW_OFF_W:RW_OFF_G]
    dg = xm[:, RW_OFF_G:]
    th, tl = _split2(jnp.tanh(xwa))
    zw = w0_ref[...] + (_dot(th, w2h_ref[...]) + _dot(th, w2l_ref[...]) + _dot(tl, w2h_ref[...]))
    za = a0_ref[...] + _dot(xwa.astype(BF16), a2_ref[...])
    g = _dot(_sigmoid(dg).astype(BF16), g2_ref[...])
    lw = -math.exp(-0.5) * _sigmoid(zw)
    asig = _sigmoid(za)
    kkv = kx * kk_ref[...]
    kkn = kkv / jnp.maximum(jnp.sqrt(_seg_sum(kkv * kkv, e_ref)), 1e-12)
    kmod = kx * (1.0 + (asig - 1.0) * ka_ref[...])
    a_ = -kkn
    b_ = kkn * asig
    k_ = kmod
    if t_valid < rows:
        valid = row1 < t_valid
        lw = jnp.where(valid, lw, 0.0)
        b_ = jnp.where(valid, b_, 0.0)
        k_ = jnp.where(valid, k_, 0.0)
    hi, mid, lo = _split3(lw)
    cum = _dot(tri, hi) + _dot(tri, mid) + _dot(tri, lo)
    cmids = [cum[c * cc + cc // 2 - 1:c * cc + cc // 2, :] for c in range(n_chunks)]
    cmid = jnp.concatenate([jnp.broadcast_to(m, (cc, RW_WIDTH)) for m in cmids], axis=0)
    at = (a_ * jnp.exp(cum - lw - cmid)).astype(BF16)
    rt = (r * jnp.exp(cum - cmid)).astype(BF16)
    e_neg = jnp.exp(cmid - cum)
    bt = (b_ * e_neg).astype(BF16)
    kt = (k_ * e_neg).astype(BF16)
    vb = vx.astype(BF16)

    lane = lax.broadcasted_iota(jnp.int32, (cc, LANE), 1)
    zero = jnp.zeros((cc, LANE), BF16)

    def stack(x):
        return jnp.concatenate([jnp.where(lane < HEAD_DIM, x, zero), jnp.where(lane >= HEAD_DIM, x, zero)], axis=0)

    n2 = 2 * cc
    n_pairs = RW_HEADS // 2
    steps = int(math.log2(cc))
    cps = [(c, p) for c in range(n_chunks) for p in range(n_pairs)]
    ops = {}
    for c, p in cps:
        rs, lanes = slice(c * cc, (c + 1) * cc), slice(p * LANE, (p + 1) * LANE)
        ops[c, p] = [stack(t[rs, lanes]) for t in (at, rt, bt, kt, vb)]
    mfull = {cp: _dot_nt(jnp.concatenate(ops[cp][:2], axis=0), jnp.concatenate(ops[cp][2:4], axis=0))
             for cp in cps}
    lmat = {cp: jnp.where(strict, mfull[cp][:n2, :n2], 0.0) for cp in cps}
    mrb = {cp: jnp.where(incl, mfull[cp][n2:, :n2], 0.0).astype(BF16) for cp in cps}
    x = {cp: jnp.concatenate(
        [ops[cp][0].astype(F32), _dot(jnp.where(strict, mfull[cp][:n2, n2:], 0.0).astype(BF16), ops[cp][4])],
        axis=1) for cp in cps}
    y0 = {cp: _dot(jnp.where(incl, mfull[cp][n2:, n2:], 0.0).astype(BF16), ops[cp][4]) for cp in cps}
    vk = {cp: _dot_tn(ops[cp][4], ops[cp][3]) for cp in cps}
    for s_i in range(steps):
        lb = {cp: lmat[cp].astype(BF16) for cp in cps}
        x = {cp: x[cp] + _dot(lb[cp], x[cp].astype(BF16)) for cp in cps}
        if s_i + 1 < steps:
            lmat = {cp: _dot(lb[cp], lb[cp]) for cp in cps}
    wr = {cp: jnp.concatenate([x[cp][:, :LANE].astype(BF16), ops[cp][1]], axis=0) for cp in cps}

    state = [s_ref[p] for p in range(n_pairs)]
    ychunks = []
    for c in range(n_chunks):
        g_mid = jnp.exp(cmids[c])
        g_end = jnp.exp(cum[c * cc + cc - 1:(c + 1) * cc, :] - cmids[c])
        sp = [state[p] * g_mid[:, p * LANE:(p + 1) * LANE] for p in range(n_pairs)]
        ws = [_dot_nt(wr[c, p], sp[p].astype(BF16)) for p in range(n_pairs)]
        ub = [(ws[p][:n2] + x[c, p][:, LANE:]).astype(BF16) for p in range(n_pairs)]
        y2 = [ws[p][n2:] + _dot(mrb[c, p], ub[p]) + y0[c, p] for p in range(n_pairs)]
        state = [(sp[p] + _dot_tn(ub[p], ops[c, p][2]) + vk[c, p]) * g_end[:, p * LANE:(p + 1) * LANE]
                 for p in range(n_pairs)]
        ychunks.append(jnp.concatenate([t[:cc] + t[cc:] for t in y2], axis=1))
    for p in range(n_pairs):
        s_ref[p] = state[p]
    y = jnp.concatenate(ychunks, axis=0) if n_chunks > 1 else ychunks[0]

    ym = _seg_sum(y, e_ref) * (1.0 / HEAD_DIM)
    yc = y - ym
    yv = _seg_sum(yc * yc, e_ref) * (1.0 / HEAD_DIM)
    yn = yc * lax.rsqrt(yv + GN_EPS) * lng_ref[...] + lnb_ref[...]
    bonus = _seg_sum(r * kmod * rk_ref[...], e_ref) * vx
    y_ref[0] = ((yn + bonus) * g).astype(BF16)

    @pl.when(t == pl.num_programs(1) - 1)
    def _():
        sfin_ref[0] = s_ref[...]


def _rwkv(pr, prev, s0, w, t_valid):
    batch, seq, _ = pr.shape
    rows = min(RW_ROWS, seq)
    n_pairs = RW_HEADS // 2
    vec = lambda n: _const_spec((1, n))
    return pl.pallas_call(
        functools.partial(_rwkv_kernel, n_chunks=rows // RW_CHUNK, t_valid=t_valid),
        grid=(batch, seq // rows),
        in_specs=[pl.BlockSpec((1, rows, RW_COLS), lambda b, t: (b, t, 0)),
                  pl.BlockSpec((1, 1, RW_COLS), lambda b, t: (b, 0, 0)),
                  pl.BlockSpec((1, n_pairs, LANE, LANE), lambda b, t: (b, 0, 0, 0)),
                  vec(RW_COLS), vec(RW_WIDTH), _const_spec((LANE, RW_WIDTH)), _const_spec((LANE, RW_WIDTH)),
                  vec(RW_WIDTH), _const_spec((LANE, RW_WIDTH)), _const_spec((LANE, RW_WIDTH)),
                  vec(RW_WIDTH), vec(RW_WIDTH), vec(RW_WIDTH), vec(RW_WIDTH), vec(RW_WIDTH),
                  _const_spec((LANE, LANE))],
        out_specs=[pl.BlockSpec((1, rows, RW_WIDTH), lambda b, t: (b, t, 0)),
                   pl.BlockSpec((1, n_pairs, LANE, LANE), lambda b, t: (b, 0, 0, 0))],
        out_shape=[jax.ShapeDtypeStruct((batch, seq, RW_WIDTH), BF16),
                   jax.ShapeDtypeStruct((batch, n_pairs, LANE, LANE), F32)],
        scratch_shapes=[pltpu.VMEM((n_pairs, LANE, LANE), F32), pltpu.VMEM((1, RW_COLS), F32)],
        compiler_params=_params(("parallel", "arbitrary")),
        name="rwkv7",
    )(pr, prev, s0, w["mu"], w["w0"], w["w2h"], w["w2l"], w["a0"], w["a2"], w["g2"],
      w["kk"], w["ka"], w["rk"], w["ln_g"], w["ln_b"], w["seg"])


def _even_out_kernel(x_ref, of_ref, yr_ref, wo_ref, g3_ref, gpre_ref, gpost_ref, win_ref, wout_ref,
                     o_ref, act_ref):
    mixed = _dot(of_ref[...], wo_ref[:FOX_WIDTH, :]) + _dot(yr_ref[...], wo_ref[FOX_WIDTH:, :])
    x = x_ref[...] + _rms(mixed, g3_ref[...])
    o_ref[...] = _ffn_apply(x, gpre_ref[...], gpost_ref[...], win_ref, wout_ref, act_ref)


def _even_out(x, of, yr, wo, g3, gpre, gpost, win, wout):
    n = x.shape[0]
    tm = min(ROW_TILE, n)
    vec = _const_spec((1, D_MODEL))
    return pl.pallas_call(
        _even_out_kernel,
        grid=(n // tm,),
        in_specs=[_row_spec(tm, D_MODEL), _row_spec(tm, FOX_WIDTH), _row_spec(tm, RW_WIDTH),
                  _const_spec((D_MODEL, D_MODEL)), vec, vec, vec,
                  _const_spec((D_MODEL, 2 * D_FF)), _const_spec((D_FF, D_MODEL))],
        out_specs=_row_spec(tm, D_MODEL),
        out_shape=jax.ShapeDtypeStruct((n, D_MODEL), F32),
        scratch_shapes=[pltpu.VMEM((tm, D_FF), BF16)],
        compiler_params=_params(("parallel",)),
        name="even_out_ffn",
    )(x, of, yr, wo, g3, gpre, gpost, win, wout)


def _gelu(x):
    return 0.5 * x * (1.0 + lax.erf(x * (2.0 ** -0.5)))


def _odd_kernel(x_ref, g2_ref, win_ref, lng_ref, lnb_ref, wm_ref, bs_ref, wo_ref, g3_ref,
                gpre_ref, gpost_ref, fwin_ref, fwout_ref, o_ref, *rest):
    gv_ref = rest[0] if len(rest) == 3 else None
    act_ref, gate_ref = rest[-2:]
    x = x_ref[...]
    tm = x.shape[0]
    h = _rms(x, g2_ref[...]).astype(BF16)
    v = _gelu(_dot(h, win_ref[:, GM_WIDTH:]))
    vm = jnp.mean(v, axis=-1, keepdims=True)
    vc = v - vm
    var = jnp.mean(vc * vc, axis=-1, keepdims=True)
    v = vc * lax.rsqrt(var + LN_EPS) * lng_ref[...] + lnb_ref[...]
    if gv_ref is not None:
        gv_ref[...] = v
    vb = v.astype(BF16)
    u = _gelu(_dot(h, win_ref[:, :GM_WIDTH]))
    for c in range(tm // GM_CHUNK):
        rows = slice(c * GM_CHUNK, (c + 1) * GM_CHUNK)
        sp = jnp.concatenate(
            [_dot(wm_ref[g], vb[rows, g * LANE:(g + 1) * LANE]) for g in range(GM_GROUPS)], axis=1)
        gate_ref[rows, :] = (u[rows, :] * (sp + bs_ref[...])).astype(BF16)
    mixed = _dot(gate_ref[...], wo_ref[...])
    x = x + _rms(mixed, g3_ref[...])
    o_ref[...] = _ffn_apply(x, gpre_ref[...], gpost_ref[...], fwin_ref, fwout_ref, act_ref)


def _odd(x, g2, win, lng, lnb, wm, bs, wo, g3, gpre, gpost, fwin, fwout, emit_v):
    n = x.shape[0]
    tm = min(ROW_TILE, n)
    vec = _const_spec((1, D_MODEL))
    return pl.pallas_call(
        _odd_kernel,
        grid=(n // tm,),
        in_specs=[_row_spec(tm, D_MODEL), vec, _const_spec((D_MODEL, 2 * GM_WIDTH)), vec, vec,
                  _const_spec((GM_GROUPS, GM_CHUNK, GM_CHUNK)), _const_spec((GM_CHUNK, GM_WIDTH)),
                  _const_spec((GM_WIDTH, D_MODEL)), vec, vec, vec,
                  _const_spec((D_MODEL, 2 * D_FF)), _const_spec((D_FF, D_MODEL))],
        out_specs=[_row_spec(tm, D_MODEL)] + [_row_spec(tm, GM_WIDTH)] * emit_v,
        out_shape=[jax.ShapeDtypeStruct((n, D_MODEL), F32)] + [jax.ShapeDtypeStruct((n, GM_WIDTH), F32)] * emit_v,
        scratch_shapes=[pltpu.VMEM((tm, D_FF), BF16), pltpu.VMEM((tm, GM_WIDTH), BF16)],
        compiler_params=_params(("parallel",)),
        name="odd_mixer_ffn",
    )(x, g2, win, lng, lnb, wm, bs, wo, g3, gpre, gpost, fwin, fwout)


def _pair_block_diag(s):
    b = s.shape[0]
    s = s.reshape(b, RW_HEADS // 2, 2, HEAD_DIM, HEAD_DIM)
    z = jnp.zeros_like(s[:, :, 0])
    top = jnp.concatenate([s[:, :, 0], z], axis=-1)
    bot = jnp.concatenate([z, s[:, :, 1]], axis=-1)
    return jnp.concatenate([top, bot], axis=-2)


def _pair_unblock(s):
    b = s.shape[0]
    h0 = s[:, :, :HEAD_DIM, :HEAD_DIM]
    h1 = s[:, :, HEAD_DIM:, HEAD_DIM:]
    return jnp.stack([h0, h1], axis=2).reshape(b, RW_HEADS, HEAD_DIM, HEAD_DIM)


def kernel(x_prompt, x_sample, cache_fox_k, cache_fox_v, cache_fox_logf, state_rwkv, state_rwkv_shift,
           norm_g, ffn_w_in, ffn_w_out, even_w_in, fox_bf, rw_mu, rw_w0, rw_w2, rw_a0, rw_a2, rw_g2,
           rw_kk, rw_ka, rw_rk, rw_ln_g, rw_ln_b, even_w_out, gm_w_in, gm_ln_g, gm_ln_b, gm_w_s,
           gm_b_s, gm_w_out):
    batch, seq, _ = x_prompt.shape
    dbatch, dseq, _ = x_sample.shape
    past = cache_fox_k.shape[2]
    depth = norm_g.shape[0]
    xp = x_prompt.reshape(batch * seq, D_MODEL)
    xs = x_sample.reshape(dbatch * dseq, D_MODEL)
    ng = lambda layer, i: norm_g[layer, i].reshape(1, D_MODEL)
    win = ffn_w_in.astype(BF16)
    wout = ffn_w_out.astype(BF16)

    fk_p, fv_p, fl_p, rs_p, rsh_p = [], [], [], [], []
    fk_s, fv_s, fl_s, rs_s, rsh_s = [], [], [], [], []
    gv_s = []
    for layer in range(depth):
        j = layer // 2
        if layer % 2 == 0:
            xp = _ffn(xp, ng(layer, 0), ng(layer, 1), win[layer, 0], wout[layer, 0])
            xs = _ffn(xs, ng(layer, 0), ng(layer, 1), win[layer, 0], wout[layer, 0])
            ew = even_w_in[j]
            wqkv = ew[:, :3 * FOX_WIDTH].astype(BF16)
            wf = jnp.pad(ew[:, 3 * FOX_WIDTH:FOX_COLS], ((0, 0), (0, LANE - FOX_HEADS))).astype(BF16)
            bf = jnp.pad(fox_bf[j], (0, LANE - FOX_HEADS)).reshape(1, LANE)
            wrw = ew[:, FOX_COLS:].astype(BF16)
            w2 = jnp.pad(rw_w2[j], ((0, LANE - rw_w2.shape[1]), (0, 0)))
            w2h = w2.astype(BF16)
            head_of = jnp.arange(LANE) // HEAD_DIM
            rw = dict(
                mu=rw_mu[j].reshape(1, RW_COLS), w0=rw_w0[j].reshape(1, RW_WIDTH),
                w2h=w2h, w2l=(w2 - w2h.astype(F32)).astype(BF16),
                a0=rw_a0[j].reshape(1, RW_WIDTH),
                a2=jnp.pad(rw_a2[j], ((LANE - rw_a2.shape[1], 0), (0, 0))).astype(BF16),
                g2=rw_g2[j].astype(BF16),
                kk=rw_kk[j].reshape(1, RW_WIDTH), ka=rw_ka[j].reshape(1, RW_WIDTH),
                rk=rw_rk[j].reshape(1, RW_WIDTH),
                ln_g=rw_ln_g[j].reshape(1, RW_WIDTH), ln_b=rw_ln_b[j].reshape(1, RW_WIDTH),
                seg=(head_of[:, None] == head_of[None, :]).astype(BF16))
            wo = even_w_out[j].astype(BF16)

            q, k, v, lf, lfp, pr = _proj(xp, ng(layer, 2), wqkv, wf, bf, wrw)
            of = _fox_prompt(q, k, v, lfp, batch, seq)
            pr3 = pr.reshape(batch, seq, RW_COLS)
            yr, sfin = _rwkv(pr3, jnp.zeros((batch, 1, RW_COLS), F32),
                             jnp.zeros((batch, RW_HEADS // 2, LANE, LANE), F32), rw, seq)
            xp = _even_out(xp, of, yr.reshape(batch * seq, RW_WIDTH), wo, ng(layer, 3),
                           ng(layer, 4), ng(layer, 5), win[layer, 1], wout[layer, 1])
            fk_p.append(k.reshape(batch, seq, FOX_HEADS, HEAD_DIM))
            fv_p.append(v.reshape(batch, seq, FOX_HEADS, HEAD_DIM))
            fl_p.append(lf.reshape(batch, seq, FOX_HEADS))
            rs_p.append(_pair_unblock(sfin))
            rsh_p.append(pr3[:, seq - 1:, :])

            q, k, v, lf, _, pr = _proj(xs, ng(layer, 2), wqkv, wf, bf, wrw)
            lfn_t = jnp.swapaxes(lf.reshape(dbatch, dseq, FOX_HEADS), 1, 2)
            lfc_t = jnp.swapaxes(cache_fox_logf[j], 1, 2)
            of = _fox_sample(q, k, v, lfn_t, cache_fox_k[j].reshape(dbatch, past, FOX_WIDTH),
                             cache_fox_v[j].reshape(dbatch, past, FOX_WIDTH), lfc_t, dbatch, dseq, past)
            pr3 = pr.reshape(dbatch, dseq, RW_COLS)
            pad = (-dseq) % RW_CHUNK
            yr, sfin = _rwkv(jnp.pad(pr3, ((0, 0), (0, pad), (0, 0))), state_rwkv_shift[j],
                             _pair_block_diag(state_rwkv[j]), rw, dseq)
            xs = _even_out(xs, of, yr[:, :dseq].reshape(dbatch * dseq, RW_WIDTH), wo, ng(layer, 3),
                           ng(layer, 4), ng(layer, 5), win[layer, 1], wout[layer, 1])
            fk_s.append(k.reshape(dbatch, dseq, FOX_HEADS, HEAD_DIM))
            fv_s.append(v.reshape(dbatch, dseq, FOX_HEADS, HEAD_DIM))
            fl_s.append(lf.reshape(dbatch, dseq, FOX_HEADS))
            rs_s.append(_pair_unblock(sfin))
            rsh_s.append(pr3[:, dseq - 1:, :])
        else:
            xp = _ffn(xp, ng(layer, 0), ng(layer, 1), win[layer, 0], wout[layer, 0])
            xs = _ffn(xs, ng(layer, 0), ng(layer, 1), win[layer, 0], wout[layer, 0])
            cpos = jnp.arange(GM_CHUNK) // CHUNK
            wm = jnp.where(cpos[None, :] <= cpos[:, None], gm_w_s[j], 0.0)
            bs = jnp.repeat(gm_b_s[j].T, GM_WIDTH // GM_GROUPS, axis=1)
            reps = GM_CHUNK // dseq
            wm_s = jnp.einsum('ab,gts->gatbs', jnp.eye(reps, dtype=F32), wm[:, :dseq, :dseq])
            wm_s = wm_s.reshape(GM_GROUPS, GM_CHUNK, GM_CHUNK)
            bs_s = jnp.tile(bs[:dseq], (reps, 1))
            odd = (ng(layer, 2), gm_w_in[j].astype(BF16), gm_ln_g[j].reshape(1, GM_WIDTH),
                   gm_ln_b[j].reshape(1, GM_WIDTH))
            tail = (gm_w_out[j].astype(BF16), ng(layer, 3), ng(layer, 4), ng(layer, 5),
                    win[layer, 1], wout[layer, 1])
            xp, = _odd(xp, *odd, wm.astype(BF16), bs, *tail, emit_v=False)
            xs, gv = _odd(xs, *odd, wm_s.astype(BF16), bs_s, *tail, emit_v=True)
            gv_s.append(gv.reshape(dbatch, dseq, GM_WIDTH))
    return (xp.reshape(batch, seq, D_MODEL), xs.reshape(dbatch, dseq, D_MODEL),
            jnp.stack(fk_p), jnp.stack(fv_p), jnp.stack(fl_p), jnp.stack(rs_p), jnp.stack(rsh_p),
            jnp.stack(fk_s), jnp.stack(fv_s), jnp.stack(fl_s), jnp.stack(rs_s), jnp.stack(rsh_s),
            jnp.stack(gv_s))
```

```python
import functools
import math

import jax
import jax.numpy as jnp
from jax import lax
from jax.experimental import pallas as pl
from jax.experimental.pallas import tpu as pltpu

F32 = jnp.float32
BF16 = jnp.bfloat16

D_MODEL = 1024
D_FF = 2816
HEAD_DIM = 64
FOX_WIDTH = 512
FOX_HEADS = 8
RW_WIDTH = 512
RW_HEADS = 8
RW_COLS = 1792
RW_OFF_W = 1536
RW_OFF_G = 1664
FOX_COLS = 3 * FOX_WIDTH + FOX_HEADS
GM_WIDTH = 1024
GM_CHUNK = 128
GM_GROUPS = 8
CHUNK = 64
FFN_RES_SCALE = 0.5
RMS_EPS = 1e-6
LN_EPS = 1e-5
GN_EPS = 64e-5

LANE = 128
ROW_TILE = 512
FF_CHUNK = 256
FOX_BLOCK = 256
FOX_VROWS = HEAD_DIM + 16
LOG2E = 1.4426950408889634
RW_CHUNK = 64
RW_ROWS = 256
NEG_BIG = -1e30
VMEM_LIMIT = 56 * 1024 * 1024


def _params(sem):
    return pltpu.CompilerParams(dimension_semantics=sem, vmem_limit_bytes=VMEM_LIMIT)


def _const_spec(shape):
    n = len(shape)
    return pl.BlockSpec(shape, lambda *_: (0,) * n, pipeline_mode=pl.Buffered(1))


def _dot(a, b):
    return jnp.dot(a, b, preferred_element_type=F32)


def _dot_nt(a, b):
    return lax.dot_general(a, b, (((1,), (1,)), ((), ())), preferred_element_type=F32)


def _dot_tn(a, b):
    return lax.dot_general(a, b, (((0,), (0,)), ((), ())), preferred_element_type=F32)


def _split2(x):
    hi = x.astype(BF16)
    lo = (x - hi.astype(F32)).astype(BF16)
    return hi, lo


def _split3(x):
    hi = x.astype(BF16)
    r1 = x - hi.astype(F32)
    mid = r1.astype(BF16)
    lo = (r1 - mid.astype(F32)).astype(BF16)
    return hi, mid, lo


def _rms(x, g):
    return x * lax.rsqrt(jnp.mean(x * x, axis=-1, keepdims=True) + RMS_EPS) * g


def _sigmoid(x):
    return 1.0 / (1.0 + jnp.exp(-x))


def _log_sigmoid(z):
    return jnp.minimum(z, 0.0) - jnp.log1p(jnp.exp(-jnp.abs(z)))


def _ffn_apply(x, gpre, gpost, win_ref, wout_ref, act_ref):
    h = _rms(x, gpre).astype(BF16)
    for c in range(D_FF // FF_CHUNK):
        lo = c * FF_CHUNK
        gate = _dot(h, win_ref[:, lo:lo + FF_CHUNK])
        up = _dot(h, win_ref[:, D_FF + lo:D_FF + lo + FF_CHUNK])
        act_ref[:, lo:lo + FF_CHUNK] = (gate * _sigmoid(gate) * up).astype(BF16)
    y = _dot(act_ref[...], wout_ref[...])
    return x + FFN_RES_SCALE * _rms(y, gpost)


def _ffn_kernel(x_ref, gpre_ref, gpost_ref, win_ref, wout_ref, o_ref, act_ref):
    o_ref[...] = _ffn_apply(x_ref[...], gpre_ref[...], gpost_ref[...], win_ref, wout_ref, act_ref)


def _row_spec(tm, width):
    return pl.BlockSpec((tm, width), lambda i: (i, 0))


def _ffn(x, gpre, gpost, win, wout):
    n = x.shape[0]
    tm = min(ROW_TILE, n)
    return pl.pallas_call(
        _ffn_kernel,
        grid=(n // tm,),
        in_specs=[_row_spec(tm, D_MODEL), _const_spec((1, D_MODEL)), _const_spec((1, D_MODEL)),
                  _const_spec((D_MODEL, 2 * D_FF)), _const_spec((D_FF, D_MODEL))],
        out_specs=_row_spec(tm, D_MODEL),
        out_shape=jax.ShapeDtypeStruct((n, D_MODEL), F32),
        scratch_shapes=[pltpu.VMEM((tm, D_FF), BF16)],
        compiler_params=_params(("parallel",)),
        name="ffn",
    )(x, gpre, gpost, win, wout)


def _head_major(x):
    per_head = []
    for g in range(x.shape[1] // LANE):
        xg = x[:, g * LANE:(g + 1) * LANE]
        per_head += [xg, pltpu.roll(xg, HEAD_DIM, 1)]
    return pltpu.einshape("htd->thd", jnp.stack(per_head, axis=0))[:, :, :HEAD_DIM]


def _proj_kernel(x_ref, g_ref, wqkv_ref, wf_ref, bf_ref, wrw_ref,
                 q_ref, kb_ref, vb_ref, k_ref, v_ref, lf_ref, lfp_ref, pr_ref):
    h = _rms(x_ref[...], g_ref[...]).astype(BF16)
    q_ref[...] = _dot(h, wqkv_ref[:, :FOX_WIDTH]).astype(BF16)
    k = _dot(h, wqkv_ref[:, FOX_WIDTH:2 * FOX_WIDTH])
    v = _dot(h, wqkv_ref[:, 2 * FOX_WIDTH:])
    kb_ref[...] = k.astype(BF16)
    vb_ref[...] = v.astype(BF16)
    k_ref[...] = _head_major(k)
    v_ref[...] = _head_major(v)
    lf = _log_sigmoid(_dot(h, wf_ref[...]) + bf_ref[...])
    lf_ref[...] = lf[:, :FOX_HEADS]
    lfp_ref[...] = lf
    pr_ref[...] = _dot(h, wrw_ref[...])


def _proj(x, g, wqkv, wf, bf, wrw):
    n = x.shape[0]
    tm = min(ROW_TILE, n)
    return pl.pallas_call(
        _proj_kernel,
        grid=(n // tm,),
        in_specs=[_row_spec(tm, D_MODEL), _const_spec((1, D_MODEL)),
                  _const_spec((D_MODEL, 3 * FOX_WIDTH)), _const_spec((D_MODEL, LANE)),
                  _const_spec((1, LANE)), _const_spec((D_MODEL, RW_COLS))],
        out_specs=[_row_spec(tm, FOX_WIDTH), _row_spec(tm, FOX_WIDTH), _row_spec(tm, FOX_WIDTH),
                   pl.BlockSpec((tm, FOX_HEADS, HEAD_DIM), lambda i: (i, 0, 0)),
                   pl.BlockSpec((tm, FOX_HEADS, HEAD_DIM), lambda i: (i, 0, 0)),
                   _row_spec(tm, FOX_HEADS), _row_spec(tm, LANE), _row_spec(tm, RW_COLS)],
        out_shape=[jax.ShapeDtypeStruct((n, FOX_WIDTH), BF16),
                   jax.ShapeDtypeStruct((n, FOX_WIDTH), BF16),
                   jax.ShapeDtypeStruct((n, FOX_WIDTH), BF16),
                   jax.ShapeDtypeStruct((n, FOX_HEADS, HEAD_DIM), F32),
                   jax.ShapeDtypeStruct((n, FOX_HEADS, HEAD_DIM), F32),
                   jax.ShapeDtypeStruct((n, FOX_HEADS), F32),
                   jax.ShapeDtypeStruct((n, LANE), F32),
                   jax.ShapeDtypeStruct((n, RW_COLS), F32)],
        compiler_params=_params(("parallel",)),
        name="even_proj",
    )(x, g, wqkv, wf, bf, wrw)


def _lane_cumsum(rows, carry):
    r = lax.broadcasted_iota(jnp.int32, (LANE, LANE), 0)
    c = lax.broadcasted_iota(jnp.int32, (LANE, LANE), 1)
    upper = (r <= c).astype(BF16)
    hi, mid, lo = _split3(rows)
    return _dot(hi, upper) + _dot(mid, upper) + _dot(lo, upper) + carry


def _fox_prompt_kernel(q_ref, k_ref, v_ref, lfp_ref, o_ref, kx_ref, vt_ref, qx_ref, m_ref, alpha_ref, p_ref,
                       acc_ref, *, seq):
    i = pl.program_id(1)
    bq = FOX_BLOCK
    n_pairs = FOX_HEADS // 2
    heads = range(FOX_HEADS)
    val_rows = [slice(h * FOX_VROWS, h * FOX_VROWS + HEAD_DIM) for h in heads]
    ext_rows = [slice(h * FOX_VROWS, (h + 1) * FOX_VROWS) for h in heads]

    @pl.when(i == 0)
    def _():
        r = lax.broadcasted_iota(jnp.int32, (LANE, LANE), 0)
        c = lax.broadcasted_iota(jnp.int32, (LANE, LANE), 1)
        tri = (c <= r).astype(BF16)
        ones_row = (r[:FOX_VROWS - HEAD_DIM] == 0).astype(BF16)
        carry = jnp.zeros((1, LANE), F32)
        for blk in range(seq // LANE):
            rows = slice(blk * LANE, (blk + 1) * LANE)
            hi, mid, lo = _split3(jnp.where(c < FOX_HEADS, lfp_ref[rows, :], 0.0))
            cum = _dot(tri, hi) + _dot(tri, mid) + _dot(tri, lo) + carry
            carry = cum[LANE - 1:LANE, :]
            chi, cmid, clo = _split3(cum * LOG2E)
            feat = (chi.astype(F32) + pltpu.roll(cmid.astype(F32), FOX_HEADS, 1)
                    + pltpu.roll(clo.astype(F32), 2 * FOX_HEADS, 1)).astype(BF16)
            for p in range(n_pairs):
                lanes = slice(p * LANE, (p + 1) * LANE)
                kx_ref[p, rows, :LANE] = k_ref[rows, lanes]
                kx_ref[p, rows, LANE:] = feat
                vt = v_ref[rows, lanes].astype(F32).T.astype(BF16)
                vt_ref[val_rows[2 * p], rows] = vt[:HEAD_DIM]
                vt_ref[val_rows[2 * p + 1], rows] = vt[HEAD_DIM:]
            for h in heads:
                vt_ref[h * FOX_VROWS + HEAD_DIM:(h + 1) * FOX_VROWS, rows] = ones_row

    q0 = pl.multiple_of(i * bq, bq)
    lane = lax.broadcasted_iota(jnp.int32, (1, LANE), 1)
    krow = lax.broadcasted_iota(jnp.int32, (bq, bq), 0)
    qcol = lax.broadcasted_iota(jnp.int32, (bq, bq), 1)
    causal = krow <= qcol
    for p in range(n_pairs):
        qp = q_ref[:, p * LANE:(p + 1) * LANE].astype(F32) * (HEAD_DIM ** -0.5 * LOG2E)
        for hh in range(2):
            h = 2 * p + hh
            in_head = (lane >= HEAD_DIM) if hh else (lane < HEAD_DIM)
            pick = (lane == h) | (lane == h + FOX_HEADS) | (lane == h + 2 * FOX_HEADS)
            sel = jnp.where(pick, -1.0, 0.0).astype(BF16)
            qx_ref[h, :, :LANE] = jnp.where(in_head, qp, 0.0).astype(BF16)
            qx_ref[h, :, LANE:] = jnp.broadcast_to(sel, (bq, LANE))

    def scores(j0):
        return [_dot_nt(kx_ref[h // 2, pl.ds(j0, bq), :], qx_ref[h]) for h in heads]

    def softmax(sts, diag):
        for h in heads:
            st = jnp.where(causal, sts[h], NEG_BIG) if diag else sts[h]
            m_old = m_ref[h:h + 1, :]
            m_new = jnp.maximum(m_old, jnp.max(st, axis=0, keepdims=True))
            alpha_ref[h:h + 1, :] = jnp.exp2(m_old - m_new)
            m_ref[h:h + 1, :] = m_new
            p_ref[h] = jnp.exp2(st - m_new).astype(BF16)

    def accumulate(j0):
        pvs = [_dot(vt_ref[ext_rows[h], pl.ds(j0, bq)], p_ref[h]) for h in heads]
        for h in heads:
            acc_ref[ext_rows[h], :] = alpha_ref[h:h + 1, :] * acc_ref[ext_rows[h], :] + pvs[h]

    m_ref[...] = jnp.full(m_ref.shape, NEG_BIG, F32)
    acc_ref[...] = jnp.zeros(acc_ref.shape, F32)
    softmax(scores(q0), True)

    def body(j, _):
        j0 = pl.multiple_of(j * bq, bq)
        sts = scores(j0)
        accumulate(pl.multiple_of(jnp.where(j == 0, q0, j0 - bq), bq))
        softmax(sts, False)
        return 0

    lax.fori_loop(0, i, body, 0)
    accumulate(pl.multiple_of(jnp.where(i == 0, q0, q0 - bq), bq))
    out = [acc_ref[val_rows[h], :] * (1.0 / acc_ref[h * FOX_VROWS + HEAD_DIM:h * FOX_VROWS + HEAD_DIM + 1, :])
           for h in heads]
    o_ref[...] = jnp.concatenate(out, axis=0).T.astype(BF16)


def _fox_prompt(q, k, v, lfp, batch, seq):
    nq = seq // FOX_BLOCK
    return pl.pallas_call(
        functools.partial(_fox_prompt_kernel, seq=seq),
        grid=(batch, nq),
        in_specs=[pl.BlockSpec((FOX_BLOCK, FOX_WIDTH), lambda b, i: (b * nq + i, 0)),
                  pl.BlockSpec((seq, FOX_WIDTH), lambda b, i: (b, 0)),
                  pl.BlockSpec((seq, FOX_WIDTH), lambda b, i: (b, 0)),
                  pl.BlockSpec((seq, LANE), lambda b, i: (b, 0))],
        out_specs=pl.BlockSpec((FOX_BLOCK, FOX_WIDTH), lambda b, i: (b * nq + i, 0)),
        out_shape=jax.ShapeDtypeStruct((batch * seq, FOX_WIDTH), BF16),
        scratch_shapes=[pltpu.VMEM((FOX_HEADS // 2, seq, 2 * LANE), BF16),
                        pltpu.VMEM((FOX_HEADS * FOX_VROWS, seq), BF16),
                        pltpu.VMEM((FOX_HEADS, FOX_BLOCK, 2 * LANE), BF16),
                        pltpu.VMEM((FOX_HEADS, FOX_BLOCK), F32),
                        pltpu.VMEM((FOX_HEADS, FOX_BLOCK), F32),
                        pltpu.VMEM((FOX_HEADS, FOX_BLOCK, FOX_BLOCK), BF16),
                        pltpu.VMEM((FOX_HEADS * FOX_VROWS, FOX_BLOCK), F32)],
        compiler_params=_params(("parallel", "arbitrary")),
        name="fox_prompt",
    )(q, k, v, lfp)


def _fox_sample_kernel(q_ref, kn_ref, vn_ref, lfn_ref, kc_ref, vc_ref, lfc_ref, o_ref,
                       kb_ref, vb_ref, s_ref, *, past, tnew):
    rows = FOX_HEADS * tnew
    q = q_ref[...].astype(F32) * (HEAD_DIM ** -0.5)
    lane_head = lax.broadcasted_iota(jnp.int32, (tnew, FOX_WIDTH), 1) // HEAD_DIM
    qs = jnp.concatenate([jnp.where(lane_head == h, q, 0.0) for h in range(FOX_HEADS)], axis=0).astype(BF16)

    blk = 512
    def cast(c, _):
        r0 = pl.multiple_of(c * blk, blk)
        kb_ref[pl.ds(r0, blk), :] = kc_ref[0, pl.ds(r0, blk), :].astype(BF16)
        vb_ref[pl.ds(r0, blk), :] = vc_ref[0, pl.ds(r0, blk), :].astype(BF16)
        return 0
    lax.fori_loop(0, past // blk, cast, 0)

    carry = jnp.zeros((FOX_HEADS, 1), F32)
    cts = []
    for b in range(past // LANE):
        c = _lane_cumsum(lfc_ref[0, :, b * LANE:(b + 1) * LANE], carry)
        cts.append(c)
        carry = c[:, LANE - 1:LANE]
    total = carry
    r = lax.broadcasted_iota(jnp.int32, (tnew, tnew), 0)
    c = lax.broadcasted_iota(jnp.int32, (tnew, tnew), 1)
    hi, mid, lo = _split3(lfn_ref[0])
    upper = (r <= c).astype(BF16)
    cnew = _dot(hi, upper) + _dot(mid, upper) + _dot(lo, upper)

    for b in range(past // LANE):
        sl = slice(b * LANE, (b + 1) * LANE)
        s = _dot_nt(qs, kb_ref[sl, :])
        ck = cts[b] - total
        bias = jnp.concatenate([jnp.broadcast_to(ck[h:h + 1, :], (tnew, LANE)) for h in range(FOX_HEADS)], axis=0)
        s_ref[:, sl] = s - bias
    sn = _dot_nt(qs, kn_ref[...].astype(BF16))
    bias_n = jnp.concatenate([jnp.broadcast_to(cnew[h:h + 1, :], (tnew, tnew)) for h in range(FOX_HEADS)], axis=0)
    tq = lax.broadcasted_iota(jnp.int32, (rows, tnew), 0) % tnew
    tk = lax.broadcasted_iota(jnp.int32, (rows, tnew), 1)
    sn = jnp.where(tk <= tq, sn - bias_n, NEG_BIG)

    sc = s_ref[...]
    m = jnp.maximum(jnp.max(sc, axis=1, keepdims=True), jnp.max(sn, axis=1, keepdims=True))
    pc = jnp.exp(sc - m)
    pn = jnp.exp(sn - m)
    l = jnp.sum(pc, axis=1, keepdims=True) + jnp.sum(pn, axis=1, keepdims=True)
    o = (_dot(pc.astype(BF16), vb_ref[...]) + _dot(pn.astype(BF16), vn_ref[...].astype(BF16))) / l
    out = jnp.zeros((tnew, FOX_WIDTH), F32)
    for h in range(FOX_HEADS):
        out = out + jnp.where(lane_head == h, o[h * tnew:(h + 1) * tnew, :], 0.0)
    o_ref[...] = out.astype(BF16)


def _fox_sample(q, kn, vn, lfn_t, kc, vc, lfc_t, batch, tnew, past):
    return pl.pallas_call(
        functools.partial(_fox_sample_kernel, past=past, tnew=tnew),
        grid=(batch,),
        in_specs=[pl.BlockSpec((tnew, FOX_WIDTH), lambda b: (b, 0)),
                  pl.BlockSpec((tnew, FOX_WIDTH), lambda b: (b, 0)),
                  pl.BlockSpec((tnew, FOX_WIDTH), lambda b: (b, 0)),
                  pl.BlockSpec((1, FOX_HEADS, tnew), lambda b: (b, 0, 0)),
                  pl.BlockSpec((1, past, FOX_WIDTH), lambda b: (b, 0, 0)),
                  pl.BlockSpec((1, past, FOX_WIDTH), lambda b: (b, 0, 0)),
                  pl.BlockSpec((1, FOX_HEADS, past), lambda b: (b, 0, 0))],
        out_specs=pl.BlockSpec((tnew, FOX_WIDTH), lambda b: (b, 0)),
        out_shape=jax.ShapeDtypeStruct((batch * tnew, FOX_WIDTH), BF16),
        scratch_shapes=[pltpu.VMEM((past, FOX_WIDTH), BF16), pltpu.VMEM((past, FOX_WIDTH), BF16),
                        pltpu.VMEM((FOX_HEADS * tnew, past), F32)],
        compiler_params=_params(("parallel",)),
        name="fox_sample",
    )(q, kn, vn, lfn_t, kc, vc, lfc_t)


def _seg_sum(z, e_ref):
    zb = z.astype(BF16)
    return jnp.concatenate(
        [_dot(zb[:, p * LANE:(p + 1) * LANE], e_ref[...]) for p in range(z.shape[1] // LANE)], axis=1)


def _rwkv_kernel(pr_ref, prev_ref, s0_ref, mu_ref, w0_ref, w2h_ref, w2l_ref, a0_ref, a2_ref, g2_ref,
                 kk_ref, ka_ref, rk_ref, lng_ref, lnb_ref, e_ref,
                 y_ref, sfin_ref, s_ref, pv_ref, *, n_chunks, t_valid):
    cc = RW_CHUNK
    t = pl.program_id(1)

    @pl.when(t == 0)
    def _():
        s_ref[...] = s0_ref[0]
        pv_ref[...] = prev_ref[0]

    rows = n_chunks * cc
    ri = lax.broadcasted_iota(jnp.int32, (rows, rows), 0)
    ci = lax.broadcasted_iota(jnp.int32, (rows, rows), 1)
    tri = ((ci <= ri) & (ci // cc == ri // cc)).astype(BF16)
    r2 = lax.broadcasted_iota(jnp.int32, (2 * cc, 2 * cc), 0) % cc
    c2 = lax.broadcasted_iota(jnp.int32, (2 * cc, 2 * cc), 1) % cc
    strict = c2 < r2
    incl = c2 <= r2
    row1 = lax.broadcasted_iota(jnp.int32, (rows, 1), 0)

    pr = pr_ref[0]
    prev_seq = jnp.where(row1 == 0, pv_ref[...], pltpu.roll(pr, 1, 0))
    pv_ref[...] = pr[rows - 1:rows, :]
    xm = pr + (prev_seq - pr) * mu_ref[...]
    r = xm[:, :RW_WIDTH]
    kx = xm[:, RW_WIDTH:2 * RW_WIDTH]
    vx = xm[:, 2 * RW_WIDTH:RW_OFF_W]
    xwa = xm[:, RW_OFF_W:RW_OFF_G]
    dg = xm[:, RW_OFF_G:]
    th, tl = _split2(jnp.tanh(xwa))
    zw = w0_ref[...] + (_dot(th, w2h_ref[...]) + _dot(th, w2l_ref[...]) + _dot(tl, w2h_ref[...]))
    za = a0_ref[...] + _dot(xwa.astype(BF16), a2_ref[...])
    g = _dot(_sigmoid(dg).astype(BF16), g2_ref[...])
    lw = -math.exp(-0.5) * _sigmoid(zw)
    asig = _sigmoid(za)
    kkv = kx * kk_ref[...]
    kkn = kkv / jnp.maximum(jnp.sqrt(_seg_sum(kkv * kkv, e_ref)), 1e-12)
    kmod = kx * (1.0 + (asig - 1.0) * ka_ref[...])
    a_ = -kkn
    b_ = kkn * asig
    k_ = kmod
    if t_valid < rows:
        valid = row1 < t_valid
        lw = jnp.where(valid, lw, 0.0)
        b_ = jnp.where(valid, b_, 0.0)
        k_ = jnp.where(valid, k_, 0.0)
    hi, mid, lo = _split3(lw)
    cum = _dot(tri, hi) + _dot(tri, mid) + _dot(tri, lo)
    cmids = [cum[c * cc + cc // 2 - 1:c * cc + cc // 2, :] for c in range(n_chunks)]
    cmid = jnp.concatenate([jnp.broadcast_to(m, (cc, RW_WIDTH)) for m in cmids], axis=0)
    at = (a_ * jnp.exp(cum - lw - cmid)).astype(BF16)
    rt = (r * jnp.exp(cum - cmid)).astype(BF16)
    e_neg = jnp.exp(cmid - cum)
    bt = (b_ * e_neg).astype(BF16)
    kt = (k_ * e_neg).astype(BF16)
    vb = vx.astype(BF16)

    lane = lax.broadcasted_iota(jnp.int32, (cc, LANE), 1)
    zero = jnp.zeros((cc, LANE), BF16)

    def stack(x):
        return jnp.concatenate([jnp.where(lane < HEAD_DIM, x, zero), jnp.where(lane >= HEAD_DIM, x, zero)], axis=0)

    n2 = 2 * cc
    n_pairs = RW_HEADS // 2
    steps = int(math.log2(cc))
    cps = [(c, p) for c in range(n_chunks) for p in range(n_pairs)]
    ops = {}
    for c, p in cps:
        rs, lanes = slice(c * cc, (c + 1) * cc), slice(p * LANE, (p + 1) * LANE)
        ops[c, p] = [stack(t[rs, lanes]) for t in (at, rt, bt, kt, vb)]
    mfull = {cp: _dot_nt(jnp.concatenate(ops[cp][:2], axis=0), jnp.concatenate(ops[cp][2:4], axis=0))
             for cp in cps}
    lmat = {cp: jnp.where(strict, mfull[cp][:n2, :n2], 0.0) for cp in cps}
    mrb = {cp: jnp.where(incl, mfull[cp][n2:, :n2], 0.0).astype(BF16) for cp in cps}
    x = {cp: jnp.concatenate(
        [ops[cp][0].astype(F32), _dot(jnp.where(strict, mfull[cp][:n2, n2:], 0.0).astype(BF16), ops[cp][4])],
        axis=1) for cp in cps}
    y0 = {cp: _dot(jnp.where(incl, mfull[cp][n2:, n2:], 0.0).astype(BF16), ops[cp][4]) for cp in cps}
    vk = {cp: _dot_tn(ops[cp][4], ops[cp][3]) for cp in cps}
    for s_i in range(steps):
        lb = {cp: lmat[cp].astype(BF16) for cp in cps}
        x = {cp: x[cp] + _dot(lb[cp], x[cp].astype(BF16)) for cp in cps}
        if s_i + 1 < steps:
            lmat = {cp: _dot(lb[cp], lb[cp]) for cp in cps}
    wr = {cp: jnp.concatenate([x[cp][:, :LANE].astype(BF16), ops[cp][1]], axis=0) for cp in cps}

    state = [s_ref[p] for p in range(n_pairs)]
    ychunks = []
    for c in range(n_chunks):
        g_mid = jnp.exp(cmids[c])
        g_end = jnp.exp(cum[c * cc + cc - 1:(c + 1) * cc, :] - cmids[c])
        sp = [state[p] * g_mid[:, p * LANE:(p + 1) * LANE] for p in range(n_pairs)]
        ws = [_dot_nt(wr[c, p], sp[p].astype(BF16)) for p in range(n_pairs)]
        ub = [(ws[p][:n2] + x[c, p][:, LANE:]).astype(BF16) for p in range(n_pairs)]
        y2 = [ws[p][n2:] + _dot(mrb[c, p], ub[p]) + y0[c, p] for p in range(n_pairs)]
        state = [(sp[p] + _dot_tn(ub[p], ops[c, p][2]) + vk[c, p]) * g_end[:, p * LANE:(p + 1) * LANE]
                 for p in range(n_pairs)]
        ychunks.append(jnp.concatenate([t[:cc] + t[cc:] for t in y2], axis=1))
    for p in range(n_pairs):
        s_ref[p] = state[p]
    y = jnp.concatenate(ychunks, axis=0) if n_chunks > 1 else ychunks[0]

    ym = _seg_sum(y, e_ref) * (1.0 / HEAD_DIM)
    yc = y - ym
    yv = _seg_sum(yc * yc, e_ref) * (1.0 / HEAD_DIM)
    yn = yc * lax.rsqrt(yv + GN_EPS) * lng_ref[...] + lnb_ref[...]
    bonus = _seg_sum(r * kmod * rk_ref[...], e_ref) * vx
    y_ref[0] = ((yn + bonus) * g).astype(BF16)

    @pl.when(t == pl.num_programs(1) - 1)
    def _():
        sfin_ref[0] = s_ref[...]


def _rwkv(pr, prev, s0, w, t_valid):
    batch, seq, _ = pr.shape
    rows = min(RW_ROWS, seq)
    n_pairs = RW_HEADS // 2
    vec = lambda n: _const_spec((1, n))
    return pl.pallas_call(
        functools.partial(_rwkv_kernel, n_chunks=rows // RW_CHUNK, t_valid=t_valid),
        grid=(batch, seq // rows),
        in_specs=[pl.BlockSpec((1, rows, RW_COLS), lambda b, t: (b, t, 0)),
                  pl.BlockSpec((1, 1, RW_COLS), lambda b, t: (b, 0, 0)),
                  pl.BlockSpec((1, n_pairs, LANE, LANE), lambda b, t: (b, 0, 0, 0)),
                  vec(RW_COLS), vec(RW_WIDTH), _const_spec((LANE, RW_WIDTH)), _const_spec((LANE, RW_WIDTH)),
                  vec(RW_WIDTH), _const_spec((LANE, RW_WIDTH)), _const_spec((LANE, RW_WIDTH)),
                  vec(RW_WIDTH), vec(RW_WIDTH), vec(RW_WIDTH), vec(RW_WIDTH), vec(RW_WIDTH),
                  _const_spec((LANE, LANE))],
        out_specs=[pl.BlockSpec((1, rows, RW_WIDTH), lambda b, t: (b, t, 0)),
                   pl.BlockSpec((1, n_pairs, LANE, LANE), lambda b, t: (b, 0, 0, 0))],
        out_shape=[jax.ShapeDtypeStruct((batch, seq, RW_WIDTH), BF16),
                   jax.ShapeDtypeStruct((batch, n_pairs, LANE, LANE), F32)],
        scratch_shapes=[pltpu.VMEM((n_pairs, LANE, LANE), F32), pltpu.VMEM((1, RW_COLS), F32)],
        compiler_params=_params(("parallel", "arbitrary")),
        name="rwkv7",
    )(pr, prev, s0, w["mu"], w["w0"], w["w2h"], w["w2l"], w["a0"], w["a2"], w["g2"],
      w["kk"], w["ka"], w["rk"], w["ln_g"], w["ln_b"], w["seg"])


def _even_out_kernel(x_ref, of_ref, yr_ref, wo_ref, g3_ref, gpre_ref, gpost_ref, win_ref, wout_ref,
                     o_ref, act_ref):
    mixed = _dot(of_ref[...], wo_ref[:FOX_WIDTH, :]) + _dot(yr_ref[...], wo_ref[FOX_WIDTH:, :])
    x = x_ref[...] + _rms(mixed, g3_ref[...])
    o_ref[...] = _ffn_apply(x, gpre_ref[...], gpost_ref[...], win_ref, wout_ref, act_ref)


def _even_out(x, of, yr, wo, g3, gpre, gpost, win, wout):
    n = x.shape[0]
    tm = min(ROW_TILE, n)
    vec = _const_spec((1, D_MODEL))
    return pl.pallas_call(
        _even_out_kernel,
        grid=(n // tm,),
        in_specs=[_row_spec(tm, D_MODEL), _row_spec(tm, FOX_WIDTH), _row_spec(tm, RW_WIDTH),
                  _const_spec((D_MODEL, D_MODEL)), vec, vec, vec,
                  _const_spec((D_MODEL, 2 * D_FF)), _const_spec((D_FF, D_MODEL))],
        out_specs=_row_spec(tm, D_MODEL),
        out_shape=jax.ShapeDtypeStruct((n, D_MODEL), F32),
        scratch_shapes=[pltpu.VMEM((tm, D_FF), BF16)],
        compiler_params=_params(("parallel",)),
        name="even_out_ffn",
    )(x, of, yr, wo, g3, gpre, gpost, win, wout)


def _gelu(x):
    return 0.5 * x * (1.0 + lax.erf(x * (2.0 ** -0.5)))


def _odd_kernel(x_ref, g2_ref, win_ref, lng_ref, lnb_ref, wm_ref, bs_ref, wo_ref, g3_ref,
                gpre_ref, gpost_ref, fwin_ref, fwout_ref, o_ref, *rest):
    gv_ref = rest[0] if len(rest) == 3 else None
    act_ref, gate_ref = rest[-2:]
    x = x_ref[...]
    tm = x.shape[0]
    h = _rms(x, g2_ref[...]).astype(BF16)
    v = _gelu(_dot(h, win_ref[:, GM_WIDTH:]))
    vm = jnp.mean(v, axis=-1, keepdims=True)
    vc = v - vm
    var = jnp.mean(vc * vc, axis=-1, keepdims=True)
    v = vc * lax.rsqrt(var + LN_EPS) * lng_ref[...] + lnb_ref[...]
    if gv_ref is not None:
        gv_ref[...] = v
    vb = v.astype(BF16)
    u = _gelu(_dot(h, win_ref[:, :GM_WIDTH]))
    for c in range(tm // GM_CHUNK):
        rows = slice(c * GM_CHUNK, (c + 1) * GM_CHUNK)
        sp = jnp.concatenate(
            [_dot(wm_ref[g], vb[rows, g * LANE:(g + 1) * LANE]) for g in range(GM_GROUPS)], axis=1)
        gate_ref[rows, :] = (u[rows, :] * (sp + bs_ref[...])).astype(BF16)
    mixed = _dot(gate_ref[...], wo_ref[...])
    x = x + _rms(mixed, g3_ref[...])
    o_ref[...] = _ffn_apply(x, gpre_ref[...], gpost_ref[...], fwin_ref, fwout_ref, act_ref)


def _odd(x, g2, win, lng, lnb, wm, bs, wo, g3, gpre, gpost, fwin, fwout, emit_v):
    n = x.shape[0]
    tm = min(ROW_TILE, n)
    vec = _const_spec((1, D_MODEL))
    return pl.pallas_call(
        _odd_kernel,
        grid=(n // tm,),
        in_specs=[_row_spec(tm, D_MODEL), vec, _const_spec((D_MODEL, 2 * GM_WIDTH)), vec, vec,
                  _const_spec((GM_GROUPS, GM_CHUNK, GM_CHUNK)), _const_spec((GM_CHUNK, GM_WIDTH)),
                  _const_spec((GM_WIDTH, D_MODEL)), vec, vec, vec,
                  _const_spec((D_MODEL, 2 * D_FF)), _const_spec((D_FF, D_MODEL))],
        out_specs=[_row_spec(tm, D_MODEL)] + [_row_spec(tm, GM_WIDTH)] * emit_v,
        out_shape=[jax.ShapeDtypeStruct((n, D_MODEL), F32)] + [jax.ShapeDtypeStruct((n, GM_WIDTH), F32)] * emit_v,
        scratch_shapes=[pltpu.VMEM((tm, D_FF), BF16), pltpu.VMEM((tm, GM_WIDTH), BF16)],
        compiler_params=_params(("parallel",)),
        name="odd_mixer_ffn",
    )(x, g2, win, lng, lnb, wm, bs, wo, g3, gpre, gpost, fwin, fwout)


def _pair_block_diag(s):
    b = s.shape[0]
    s = s.reshape(b, RW_HEADS // 2, 2, HEAD_DIM, HEAD_DIM)
    z = jnp.zeros_like(s[:, :, 0])
    top = jnp.concatenate([s[:, :, 0], z], axis=-1)
    bot = jnp.concatenate([z, s[:, :, 1]], axis=-1)
    return jnp.concatenate([top, bot], axis=-2)


def _pair_unblock(s):
    b = s.shape[0]
    h0 = s[:, :, :HEAD_DIM, :HEAD_DIM]
    h1 = s[:, :, HEAD_DIM:, HEAD_DIM:]
    return jnp.stack([h0, h1], axis=2).reshape(b, RW_HEADS, HEAD_DIM, HEAD_DIM)


def kernel(x_prompt, x_sample, cache_fox_k, cache_fox_v, cache_fox_logf, state_rwkv, state_rwkv_shift,
           norm_g, ffn_w_in, ffn_w_out, even_w_in, fox_bf, rw_mu, rw_w0, rw_w2, rw_a0, rw_a2, rw_g2,
           rw_kk, rw_ka, rw_rk, rw_ln_g, rw_ln_b, even_w_out, gm_w_in, gm_ln_g, gm_ln_b, gm_w_s,
           gm_b_s, gm_w_out):
    batch, seq, _ = x_prompt.shape
    dbatch, dseq, _ = x_sample.shape
    past = cache_fox_k.shape[2]
    depth = norm_g.shape[0]
    xp = x_prompt.reshape(batch * seq, D_MODEL)
    xs = x_sample.reshape(dbatch * dseq, D_MODEL)
    ng = lambda layer, i: norm_g[layer, i].reshape(1, D_MODEL)
    win = ffn_w_in.astype(BF16)
    wout = ffn_w_out.astype(BF16)

    fk_p, fv_p, fl_p, rs_p, rsh_p = [], [], [], [], []
    fk_s, fv_s, fl_s, rs_s, rsh_s = [], [], [], [], []
    gv_s = []
    for layer in range(depth):
        j = layer // 2
        if layer % 2 == 0:
            xp = _ffn(xp, ng(layer, 0), ng(layer, 1), win[layer, 0], wout[layer, 0])
            xs = _ffn(xs, ng(layer, 0), ng(layer, 1), win[layer, 0], wout[layer, 0])
            ew = even_w_in[j]
            wqkv = ew[:, :3 * FOX_WIDTH].astype(BF16)
            wf = jnp.pad(ew[:, 3 * FOX_WIDTH:FOX_COLS], ((0, 0), (0, LANE - FOX_HEADS))).astype(BF16)
            bf = jnp.pad(fox_bf[j], (0, LANE - FOX_HEADS)).reshape(1, LANE)
            wrw = ew[:, FOX_COLS:].astype(BF16)
            w2 = jnp.pad(rw_w2[j], ((0, LANE - rw_w2.shape[1]), (0, 0)))
            w2h = w2.astype(BF16)
            head_of = jnp.arange(LANE) // HEAD_DIM
            rw = dict(
                mu=rw_mu[j].reshape(1, RW_COLS), w0=rw_w0[j].reshape(1, RW_WIDTH),
                w2h=w2h, w2l=(w2 - w2h.astype(F32)).astype(BF16),
                a0=rw_a0[j].reshape(1, RW_WIDTH),
                a2=jnp.pad(rw_a2[j], ((LANE - rw_a2.shape[1], 0), (0, 0))).astype(BF16),
                g2=rw_g2[j].astype(BF16),
                kk=rw_kk[j].reshape(1, RW_WIDTH), ka=rw_ka[j].reshape(1, RW_WIDTH),
                rk=rw_rk[j].reshape(1, RW_WIDTH),
                ln_g=rw_ln_g[j].reshape(1, RW_WIDTH), ln_b=rw_ln_b[j].reshape(1, RW_WIDTH),
                seg=(head_of[:, None] == head_of[None, :]).astype(BF16))
            wo = even_w_out[j].astype(BF16)

            q, kb, vb, k, v, lf, lfp, pr = _proj(xp, ng(layer, 2), wqkv, wf, bf, wrw)
            of = _fox_prompt(q, kb, vb, lfp, batch, seq)
            pr3 = pr.reshape(batch, seq, RW_COLS)
            yr, sfin = _rwkv(pr3, jnp.zeros((batch, 1, RW_COLS), F32),
                             jnp.zeros((batch, RW_HEADS // 2, LANE, LANE), F32), rw, seq)
            xp = _even_out(xp, of, yr.reshape(batch * seq, RW_WIDTH), wo, ng(layer, 3),
                           ng(layer, 4), ng(layer, 5), win[layer, 1], wout[layer, 1])
            fk_p.append(k.reshape(batch, seq, FOX_HEADS, HEAD_DIM))
            fv_p.append(v.reshape(batch, seq, FOX_HEADS, HEAD_DIM))
            fl_p.append(lf.reshape(batch, seq, FOX_HEADS))
            rs_p.append(_pair_unblock(sfin))
            rsh_p.append(pr3[:, seq - 1:, :])

            q, kb, vb, k, v, lf, _, pr = _proj(xs, ng(layer, 2), wqkv, wf, bf, wrw)
            lfn_t = jnp.swapaxes(lf.reshape(dbatch, dseq, FOX_HEADS), 1, 2)
            lfc_t = jnp.swapaxes(cache_fox_logf[j], 1, 2)
            of = _fox_sample(q, kb, vb, lfn_t, cache_fox_k[j].reshape(dbatch, past, FOX_WIDTH),
                             cache_fox_v[j].reshape(dbatch, past, FOX_WIDTH), lfc_t, dbatch, dseq, past)
            pr3 = pr.reshape(dbatch, dseq, RW_COLS)
            pad = (-dseq) % RW_CHUNK
            yr, sfin = _rwkv(jnp.pad(pr3, ((0, 0), (0, pad), (0, 0))), state_rwkv_shift[j],
                             _pair_block_diag(state_rwkv[j]), rw, dseq)
            xs = _even_out(xs, of, yr[:, :dseq].reshape(dbatch * dseq, RW_WIDTH), wo, ng(layer, 3),
                           ng(layer, 4), ng(layer, 5), win[layer, 1], wout[layer, 1])
            fk_s.append(k.reshape(dbatch, dseq, FOX_HEADS, HEAD_DIM))
            fv_s.append(v.reshape(dbatch, dseq, FOX_HEADS, HEAD_DIM))
            fl_s.append(lf.reshape(dbatch, dseq, FOX_HEADS))
            rs_s.append(_pair_unblock(sfin))
            rsh_s.append(pr3[:, dseq - 1:, :])
        else:
            xp = _ffn(xp, ng(layer, 0), ng(layer, 1), win[layer, 0], wout[layer, 0])
            xs = _ffn(xs, ng(layer, 0), ng(layer, 1), win[layer, 0], wout[layer, 0])
            cpos = jnp.arange(GM_CHUNK) // CHUNK
            wm = jnp.where(cpos[None, :] <= cpos[:, None], gm_w_s[j], 0.0)
            bs = jnp.repeat(gm_b_s[j].T, GM_WIDTH // GM_GROUPS, axis=1)
            reps = GM_CHUNK // dseq
            wm_s = jnp.einsum('ab,gts->gatbs', jnp.eye(reps, dtype=F32), wm[:, :dseq, :dseq])
            wm_s = wm_s.reshape(GM_GROUPS, GM_CHUNK, GM_CHUNK)
            bs_s = jnp.tile(bs[:dseq], (reps, 1))
            odd = (ng(layer, 2), gm_w_in[j].astype(BF16), gm_ln_g[j].reshape(1, GM_WIDTH),
                   gm_ln_b[j].reshape(1, GM_WIDTH))
            tail = (gm_w_out[j].astype(BF16), ng(layer, 3), ng(layer, 4), ng(layer, 5),
                    win[layer, 1], wout[layer, 1])
            xp, = _odd(xp, *odd, wm.astype(BF16), bs, *tail, emit_v=False)
            xs, gv = _odd(xs, *odd, wm_s.astype(BF16), bs_s, *tail, emit_v=True)
            gv_s.append(gv.reshape(dbatch, dseq, GM_WIDTH))
    return (xp.reshape(batch, seq, D_MODEL), xs.reshape(dbatch, dseq, D_MODEL),
            jnp.stack(fk_p), jnp.stack(fv_p), jnp.stack(fl_p), jnp.stack(rs_p), jnp.stack(rsh_p),
            jnp.stack(fk_s), jnp.stack(fv_s), jnp.stack(fl_s), jnp.stack(rs_s), jnp.stack(rsh_s),
            jnp.stack(gv_s))
```

```python
import functools
import math

import jax
import jax.numpy as jnp
from jax import lax
from jax.experimental import pallas as pl
from jax.experimental.pallas import tpu as pltpu

F32 = jnp.float32
BF16 = jnp.bfloat16

D_MODEL = 1024
D_FF = 2816
HEAD_DIM = 64
FOX_WIDTH = 512
FOX_HEADS = 8
RW_WIDTH = 512
RW_HEADS = 8
RW_COLS = 1792
RW_OFF_W = 1536
RW_OFF_G = 1664
FOX_COLS = 3 * FOX_WIDTH + FOX_HEADS
GM_WIDTH = 1024
GM_CHUNK = 128
GM_GROUPS = 8
CHUNK = 64
FFN_RES_SCALE = 0.5
RMS_EPS = 1e-6
LN_EPS = 1e-5
GN_EPS = 64e-5

LANE = 128
ROW_TILE = 512
FF_CHUNK = 256
FOX_BLOCK = 256
FOX_VROWS = HEAD_DIM + 16
LOG2E = 1.4426950408889634
RW_CHUNK = 64
RW_ROWS = 256
NEG_BIG = -1e30
VMEM_LIMIT = 56 * 1024 * 1024


def _params(sem):
    return pltpu.CompilerParams(dimension_semantics=sem, vmem_limit_bytes=VMEM_LIMIT)


def _const_spec(shape):
    n = len(shape)
    return pl.BlockSpec(shape, lambda *_: (0,) * n, pipeline_mode=pl.Buffered(1))


def _dot(a, b):
    return jnp.dot(a, b, preferred_element_type=F32)


def _dot_nt(a, b):
    return lax.dot_general(a, b, (((1,), (1,)), ((), ())), preferred_element_type=F32)


def _dot_tn(a, b):
    return lax.dot_general(a, b, (((0,), (0,)), ((), ())), preferred_element_type=F32)


def _split2(x):
    hi = x.astype(BF16)
    lo = (x - hi.astype(F32)).astype(BF16)
    return hi, lo


def _split3(x):
    hi = x.astype(BF16)
    r1 = x - hi.astype(F32)
    mid = r1.astype(BF16)
    lo = (r1 - mid.astype(F32)).astype(BF16)
    return hi, mid, lo


def _rms(x, g):
    return x * lax.rsqrt(jnp.mean(x * x, axis=-1, keepdims=True) + RMS_EPS) * g


def _sigmoid(x):
    return 1.0 / (1.0 + jnp.exp(-x))


def _log_sigmoid(z):
    return jnp.minimum(z, 0.0) - jnp.log1p(jnp.exp(-jnp.abs(z)))


def _ffn_apply(x, gpre, gpost, win_ref, wout_ref, act_ref):
    h = _rms(x, gpre).astype(BF16)
    for c in range(D_FF // FF_CHUNK):
        lo = c * FF_CHUNK
        gate = _dot(h, win_ref[:, lo:lo + FF_CHUNK])
        up = _dot(h, win_ref[:, D_FF + lo:D_FF + lo + FF_CHUNK])
        act_ref[:, lo:lo + FF_CHUNK] = (gate * _sigmoid(gate) * up).astype(BF16)
    y = _dot(act_ref[...], wout_ref[...])
    return x + FFN_RES_SCALE * _rms(y, gpost)


def _ffn_kernel(x_ref, gpre_ref, gpost_ref, win_ref, wout_ref, o_ref, act_ref):
    o_ref[...] = _ffn_apply(x_ref[...], gpre_ref[...], gpost_ref[...], win_ref, wout_ref, act_ref)


def _row_spec(tm, width):
    return pl.BlockSpec((tm, width), lambda i: (i, 0))


def _ffn_weight_specs(widx):
    pick = lambda *_: (widx, 0, 0)
    return [pl.BlockSpec((None, D_MODEL, 2 * D_FF), pick, pipeline_mode=pl.Buffered(1)),
            pl.BlockSpec((None, D_FF, D_MODEL), pick, pipeline_mode=pl.Buffered(1))]


def _ffn(x, gpre, gpost, win, wout, widx):
    n = x.shape[0]
    tm = min(ROW_TILE, n)
    return pl.pallas_call(
        _ffn_kernel,
        grid=(n // tm,),
        in_specs=[_row_spec(tm, D_MODEL), _const_spec((1, D_MODEL)), _const_spec((1, D_MODEL))]
        + _ffn_weight_specs(widx),
        out_specs=_row_spec(tm, D_MODEL),
        out_shape=jax.ShapeDtypeStruct((n, D_MODEL), F32),
        scratch_shapes=[pltpu.VMEM((tm, D_FF), BF16)],
        compiler_params=_params(("parallel",)),
        name="ffn",
    )(x, gpre, gpost, win, wout)


def _head_major(x):
    per_head = []
    for g in range(x.shape[1] // LANE):
        xg = x[:, g * LANE:(g + 1) * LANE]
        per_head += [xg, pltpu.roll(xg, HEAD_DIM, 1)]
    return pltpu.einshape("htd->thd", jnp.stack(per_head, axis=0))[:, :, :HEAD_DIM]


def _proj_kernel(x_ref, g_ref, wqkv_ref, wf_ref, bf_ref, wrw_ref,
                 q_ref, kb_ref, vb_ref, k_ref, v_ref, lf_ref, lfp_ref, pr_ref):
    h = _rms(x_ref[...], g_ref[...]).astype(BF16)
    q_ref[...] = _dot(h, wqkv_ref[:, :FOX_WIDTH]).astype(BF16)
    k = _dot(h, wqkv_ref[:, FOX_WIDTH:2 * FOX_WIDTH])
    v = _dot(h, wqkv_ref[:, 2 * FOX_WIDTH:])
    kb_ref[...] = k.astype(BF16)
    vb_ref[...] = v.astype(BF16)
    k_ref[...] = _head_major(k)
    v_ref[...] = _head_major(v)
    lf = _log_sigmoid(_dot(h, wf_ref[...]) + bf_ref[...])
    lf_ref[...] = lf[:, :FOX_HEADS]
    lfp_ref[...] = lf
    pr_ref[...] = _dot(h, wrw_ref[...])


def _proj(x, g, wqkv, wf, bf, wrw):
    n = x.shape[0]
    tm = min(ROW_TILE, n)
    return pl.pallas_call(
        _proj_kernel,
        grid=(n // tm,),
        in_specs=[_row_spec(tm, D_MODEL), _const_spec((1, D_MODEL)),
                  _const_spec((D_MODEL, 3 * FOX_WIDTH)), _const_spec((D_MODEL, LANE)),
                  _const_spec((1, LANE)), _const_spec((D_MODEL, RW_COLS))],
        out_specs=[_row_spec(tm, FOX_WIDTH), _row_spec(tm, FOX_WIDTH), _row_spec(tm, FOX_WIDTH),
                   pl.BlockSpec((tm, FOX_HEADS, HEAD_DIM), lambda i: (i, 0, 0)),
                   pl.BlockSpec((tm, FOX_HEADS, HEAD_DIM), lambda i: (i, 0, 0)),
                   _row_spec(tm, FOX_HEADS), _row_spec(tm, LANE), _row_spec(tm, RW_COLS)],
        out_shape=[jax.ShapeDtypeStruct((n, FOX_WIDTH), BF16),
                   jax.ShapeDtypeStruct((n, FOX_WIDTH), BF16),
                   jax.ShapeDtypeStruct((n, FOX_WIDTH), BF16),
                   jax.ShapeDtypeStruct((n, FOX_HEADS, HEAD_DIM), F32),
                   jax.ShapeDtypeStruct((n, FOX_HEADS, HEAD_DIM), F32),
                   jax.ShapeDtypeStruct((n, FOX_HEADS), F32),
                   jax.ShapeDtypeStruct((n, LANE), F32),
                   jax.ShapeDtypeStruct((n, RW_COLS), F32)],
        compiler_params=_params(("parallel",)),
        name="even_proj",
    )(x, g, wqkv, wf, bf, wrw)


def _lane_cumsum(rows, carry):
    r = lax.broadcasted_iota(jnp.int32, (LANE, LANE), 0)
    c = lax.broadcasted_iota(jnp.int32, (LANE, LANE), 1)
    upper = (r <= c).astype(BF16)
    hi, mid, lo = _split3(rows)
    return _dot(hi, upper) + _dot(mid, upper) + _dot(lo, upper) + carry


def _fox_prompt_kernel(q_ref, k_ref, v_ref, lfp_ref, o_ref, kx_ref, vt_ref, qx_ref, m_ref, alpha_ref, p_ref,
                       acc_ref, *, seq):
    i = pl.program_id(1)
    bq = FOX_BLOCK
    n_pairs = FOX_HEADS // 2
    heads = range(FOX_HEADS)
    val_rows = [slice(h * FOX_VROWS, h * FOX_VROWS + HEAD_DIM) for h in heads]
    ext_rows = [slice(h * FOX_VROWS, (h + 1) * FOX_VROWS) for h in heads]

    @pl.when(i == 0)
    def _():
        r = lax.broadcasted_iota(jnp.int32, (LANE, LANE), 0)
        c = lax.broadcasted_iota(jnp.int32, (LANE, LANE), 1)
        tri = (c <= r).astype(BF16)
        ones_row = (r[:FOX_VROWS - HEAD_DIM] == 0).astype(BF16)
        carry = jnp.zeros((1, LANE), F32)
        for blk in range(seq // LANE):
            rows = slice(blk * LANE, (blk + 1) * LANE)
            hi, mid, lo = _split3(jnp.where(c < FOX_HEADS, lfp_ref[rows, :], 0.0))
            cum = _dot(tri, hi) + _dot(tri, mid) + _dot(tri, lo) + carry
            carry = cum[LANE - 1:LANE, :]
            chi, cmid, clo = _split3(cum * LOG2E)
            feat = (chi.astype(F32) + pltpu.roll(cmid.astype(F32), FOX_HEADS, 1)
                    + pltpu.roll(clo.astype(F32), 2 * FOX_HEADS, 1)).astype(BF16)
            for p in range(n_pairs):
                lanes = slice(p * LANE, (p + 1) * LANE)
                kx_ref[p, rows, :LANE] = k_ref[rows, lanes]
                kx_ref[p, rows, LANE:] = feat
                vt = v_ref[rows, lanes].astype(F32).T.astype(BF16)
                vt_ref[val_rows[2 * p], rows] = vt[:HEAD_DIM]
                vt_ref[val_rows[2 * p + 1], rows] = vt[HEAD_DIM:]
            for h in heads:
                vt_ref[h * FOX_VROWS + HEAD_DIM:(h + 1) * FOX_VROWS, rows] = ones_row

    q0 = pl.multiple_of(i * bq, bq)
    lane = lax.broadcasted_iota(jnp.int32, (1, LANE), 1)
    krow = lax.broadcasted_iota(jnp.int32, (bq, bq), 0)
    qcol = lax.broadcasted_iota(jnp.int32, (bq, bq), 1)
    causal = krow <= qcol
    for p in range(n_pairs):
        qp = q_ref[:, p * LANE:(p + 1) * LANE].astype(F32) * (HEAD_DIM ** -0.5 * LOG2E)
        for hh in range(2):
            h = 2 * p + hh
            in_head = (lane >= HEAD_DIM) if hh else (lane < HEAD_DIM)
            pick = (lane == h) | (lane == h + FOX_HEADS) | (lane == h + 2 * FOX_HEADS)
            sel = jnp.where(pick, -1.0, 0.0).astype(BF16)
            qx_ref[h, :, :LANE] = jnp.where(in_head, qp, 0.0).astype(BF16)
            qx_ref[h, :, LANE:] = jnp.broadcast_to(sel, (bq, LANE))

    def scores(j0):
        return [_dot_nt(kx_ref[h // 2, pl.ds(j0, bq), :], qx_ref[h]) for h in heads]

    def softmax(sts, diag):
        for h in heads:
            st = jnp.where(causal, sts[h], NEG_BIG) if diag else sts[h]
            m_old = m_ref[h:h + 1, :]
            m_new = jnp.maximum(m_old, jnp.max(st, axis=0, keepdims=True))
            alpha_ref[h:h + 1, :] = jnp.exp2(m_old - m_new)
            m_ref[h:h + 1, :] = m_new
            p_ref[h] = jnp.exp2(st - m_new).astype(BF16)

    def accumulate(j0):
        pvs = [_dot(vt_ref[ext_rows[h], pl.ds(j0, bq)], p_ref[h]) for h in heads]
        for h in heads:
            acc_ref[ext_rows[h], :] = alpha_ref[h:h + 1, :] * acc_ref[ext_rows[h], :] + pvs[h]

    m_ref[...] = jnp.full(m_ref.shape, NEG_BIG, F32)
    acc_ref[...] = jnp.zeros(acc_ref.shape, F32)
    softmax(scores(q0), True)

    def body(j, _):
        j0 = pl.multiple_of(j * bq, bq)
        sts = scores(j0)
        accumulate(pl.multiple_of(jnp.where(j == 0, q0, j0 - bq), bq))
        softmax(sts, False)
        return 0

    lax.fori_loop(0, i, body, 0)
    accumulate(pl.multiple_of(jnp.where(i == 0, q0, q0 - bq), bq))
    out = [acc_ref[val_rows[h], :] * (1.0 / acc_ref[h * FOX_VROWS + HEAD_DIM:h * FOX_VROWS + HEAD_DIM + 1, :])
           for h in heads]
    o_ref[...] = jnp.concatenate(out, axis=0).T.astype(BF16)


def _fox_prompt(q, k, v, lfp, batch, seq):
    nq = seq // FOX_BLOCK
    return pl.pallas_call(
        functools.partial(_fox_prompt_kernel, seq=seq),
        grid=(batch, nq),
        in_specs=[pl.BlockSpec((FOX_BLOCK, FOX_WIDTH), lambda b, i: (b * nq + i, 0)),
                  pl.BlockSpec((seq, FOX_WIDTH), lambda b, i: (b, 0)),
                  pl.BlockSpec((seq, FOX_WIDTH), lambda b, i: (b, 0)),
                  pl.BlockSpec((seq, LANE), lambda b, i: (b, 0))],
        out_specs=pl.BlockSpec((FOX_BLOCK, FOX_WIDTH), lambda b, i: (b * nq + i, 0)),
        out_shape=jax.ShapeDtypeStruct((batch * seq, FOX_WIDTH), BF16),
        scratch_shapes=[pltpu.VMEM((FOX_HEADS // 2, seq, 2 * LANE), BF16),
                        pltpu.VMEM((FOX_HEADS * FOX_VROWS, seq), BF16),
                        pltpu.VMEM((FOX_HEADS, FOX_BLOCK, 2 * LANE), BF16),
                        pltpu.VMEM((FOX_HEADS, FOX_BLOCK), F32),
                        pltpu.VMEM((FOX_HEADS, FOX_BLOCK), F32),
                        pltpu.VMEM((FOX_HEADS, FOX_BLOCK, FOX_BLOCK), BF16),
                        pltpu.VMEM((FOX_HEADS * FOX_VROWS, FOX_BLOCK), F32)],
        compiler_params=_params(("parallel", "arbitrary")),
        name="fox_prompt",
    )(q, k, v, lfp)


def _fox_sample_kernel(q_ref, kn_ref, vn_ref, lfn_ref, kc_ref, vc_ref, lfc_ref, o_ref,
                       kb_ref, vb_ref, s_ref, *, past, tnew):
    rows = FOX_HEADS * tnew
    q = q_ref[...].astype(F32) * (HEAD_DIM ** -0.5)
    lane_head = lax.broadcasted_iota(jnp.int32, (tnew, FOX_WIDTH), 1) // HEAD_DIM
    qs = jnp.concatenate([jnp.where(lane_head == h, q, 0.0) for h in range(FOX_HEADS)], axis=0).astype(BF16)

    blk = 512
    def cast(c, _):
        r0 = pl.multiple_of(c * blk, blk)
        for src, dst in ((kc_ref, kb_ref), (vc_ref, vb_ref)):
            z = pltpu.einshape("thd->htd", src[0, pl.ds(r0, blk)])
            for g in range(FOX_HEADS // 2):
                dst[pl.ds(r0, blk), g * LANE:(g + 1) * LANE] = jnp.concatenate(
                    [z[2 * g], z[2 * g + 1]], axis=1).astype(BF16)
        return 0
    lax.fori_loop(0, past // blk, cast, 0)

    carry = jnp.zeros((FOX_HEADS, 1), F32)
    cts = []
    for b in range(past // LANE):
        c = _lane_cumsum(lfc_ref[0, :, b * LANE:(b + 1) * LANE], carry)
        cts.append(c)
        carry = c[:, LANE - 1:LANE]
    total = carry
    r = lax.broadcasted_iota(jnp.int32, (tnew, tnew), 0)
    c = lax.broadcasted_iota(jnp.int32, (tnew, tnew), 1)
    hi, mid, lo = _split3(lfn_ref[0])
    upper = (r <= c).astype(BF16)
    cnew = _dot(hi, upper) + _dot(mid, upper) + _dot(lo, upper)

    for b in range(past // LANE):
        sl = slice(b * LANE, (b + 1) * LANE)
        s = _dot_nt(qs, kb_ref[sl, :])
        ck = cts[b] - total
        bias = jnp.concatenate([jnp.broadcast_to(ck[h:h + 1, :], (tnew, LANE)) for h in range(FOX_HEADS)], axis=0)
        s_ref[:, sl] = s - bias
    sn = _dot_nt(qs, kn_ref[...].astype(BF16))
    bias_n = jnp.concatenate([jnp.broadcast_to(cnew[h:h + 1, :], (tnew, tnew)) for h in range(FOX_HEADS)], axis=0)
    tq = lax.broadcasted_iota(jnp.int32, (rows, tnew), 0) % tnew
    tk = lax.broadcasted_iota(jnp.int32, (rows, tnew), 1)
    sn = jnp.where(tk <= tq, sn - bias_n, NEG_BIG)

    sc = s_ref[...]
    m = jnp.maximum(jnp.max(sc, axis=1, keepdims=True), jnp.max(sn, axis=1, keepdims=True))
    pc = jnp.exp(sc - m)
    pn = jnp.exp(sn - m)
    l = jnp.sum(pc, axis=1, keepdims=True) + jnp.sum(pn, axis=1, keepdims=True)
    o = (_dot(pc.astype(BF16), vb_ref[...]) + _dot(pn.astype(BF16), vn_ref[...].astype(BF16))) / l
    out = jnp.zeros((tnew, FOX_WIDTH), F32)
    for h in range(FOX_HEADS):
        out = out + jnp.where(lane_head == h, o[h * tnew:(h + 1) * tnew, :], 0.0)
    o_ref[...] = out.astype(BF16)


def _fox_sample(q, kn, vn, lfn_t, kc, vc, lfc_t, batch, tnew, past):
    return pl.pallas_call(
        functools.partial(_fox_sample_kernel, past=past, tnew=tnew),
        grid=(batch,),
        in_specs=[pl.BlockSpec((tnew, FOX_WIDTH), lambda b: (b, 0)),
                  pl.BlockSpec((tnew, FOX_WIDTH), lambda b: (b, 0)),
                  pl.BlockSpec((tnew, FOX_WIDTH), lambda b: (b, 0)),
                  pl.BlockSpec((1, FOX_HEADS, tnew), lambda b: (b, 0, 0)),
                  pl.BlockSpec((1, past, FOX_HEADS, HEAD_DIM), lambda b: (b, 0, 0, 0)),
                  pl.BlockSpec((1, past, FOX_HEADS, HEAD_DIM), lambda b: (b, 0, 0, 0)),
                  pl.BlockSpec((1, FOX_HEADS, past), lambda b: (b, 0, 0))],
        out_specs=pl.BlockSpec((tnew, FOX_WIDTH), lambda b: (b, 0)),
        out_shape=jax.ShapeDtypeStruct((batch * tnew, FOX_WIDTH), BF16),
        scratch_shapes=[pltpu.VMEM((past, FOX_WIDTH), BF16), pltpu.VMEM((past, FOX_WIDTH), BF16),
                        pltpu.VMEM((FOX_HEADS * tnew, past), F32)],
        compiler_params=_params(("parallel",)),
        name="fox_sample",
    )(q, kn, vn, lfn_t, kc, vc, lfc_t)


def _seg_sum(z, e_ref):
    zb = z.astype(BF16)
    return jnp.concatenate(
        [_dot(zb[:, p * LANE:(p + 1) * LANE], e_ref[...]) for p in range(z.shape[1] // LANE)], axis=1)


def _rwkv_kernel(pr_ref, prev_ref, s0_ref, mu_ref, w0_ref, w2h_ref, w2l_ref, a0_ref, a2_ref, g2_ref,
                 kk_ref, ka_ref, rk_ref, lng_ref, lnb_ref, e_ref,
                 y_ref, sfin_ref, s_ref, pv_ref, *, n_chunks, t_valid):
    cc = RW_CHUNK
    t = pl.program_id(1)

    @pl.when(t == 0)
    def _():
        s_ref[...] = s0_ref[0]
        pv_ref[...] = prev_ref[0]

    rows = n_chunks * cc
    half = max(n_chunks // 2, 1)
    first, second = list(range(half)), list(range(half, n_chunks))
    gr = half * cc
    ri = lax.broadcasted_iota(jnp.int32, (gr, gr), 0)
    ci = lax.broadcasted_iota(jnp.int32, (gr, gr), 1)
    tri = ((ci <= ri) & (ci // cc == ri // cc)).astype(BF16)
    r2 = lax.broadcasted_iota(jnp.int32, (2 * cc, 2 * cc), 0) % cc
    c2 = lax.broadcasted_iota(jnp.int32, (2 * cc, 2 * cc), 1) % cc
    strict = c2 < r2
    incl = c2 <= r2
    row1 = lax.broadcasted_iota(jnp.int32, (rows, 1), 0)

    pr = pr_ref[0]
    prev_seq = jnp.where(row1 == 0, pv_ref[...], pltpu.roll(pr, 1, 0))
    pv_ref[...] = pr[rows - 1:rows, :]
    xm = pr + (prev_seq - pr) * mu_ref[...]
    tok = {}
    cmids = {}

    def tw_project(chunks):
        x = xm[chunks[0] * cc:(chunks[-1] + 1) * cc]
        d = tok[chunks[0]] = dict(r=x[:, :RW_WIDTH], kx=x[:, RW_WIDTH:2 * RW_WIDTH], vx=x[:, 2 * RW_WIDTH:RW_OFF_W])
        xwa = x[:, RW_OFF_W:RW_OFF_G]
        th, tl = _split2(jnp.tanh(xwa))
        d["zw"] = w0_ref[...] + (_dot(th, w2h_ref[...]) + _dot(th, w2l_ref[...]) + _dot(tl, w2h_ref[...]))
        d["za"] = a0_ref[...] + _dot(xwa.astype(BF16), a2_ref[...])
        d["g"] = _dot(_sigmoid(x[:, RW_OFF_G:]).astype(BF16), g2_ref[...])

    def tw_decay(chunks):
        d = tok[chunks[0]]
        lw = -math.exp(-0.5) * _sigmoid(d.pop("zw"))
        asig = _sigmoid(d.pop("za"))
        kkv = d["kx"] * kk_ref[...]
        kkn = kkv / jnp.maximum(jnp.sqrt(_seg_sum(kkv * kkv, e_ref)), 1e-12)
        d["kmod"] = d.pop("kx") * (1.0 + (asig - 1.0) * ka_ref[...])
        d["a_"], d["b_"], d["k_"] = -kkn, kkn * asig, d["kmod"]
        if t_valid < rows:
            valid = row1[chunks[0] * cc:(chunks[-1] + 1) * cc] < t_valid
            lw = jnp.where(valid, lw, 0.0)
            d["b_"] = jnp.where(valid, d["b_"], 0.0)
            d["k_"] = jnp.where(valid, d["k_"], 0.0)
        hi, mid, lo = _split3(lw)
        d["lw"] = lw
        d["cum"] = _dot(tri, hi) + _dot(tri, mid) + _dot(tri, lo)

    def tw_scale(chunks):
        d = tok[chunks[0]]
        cum = d["cum"]
        for i, c in enumerate(chunks):
            cmids[c] = cum[i * cc + cc // 2 - 1:i * cc + cc // 2, :]
        cmid = jnp.concatenate([jnp.broadcast_to(cmids[c], (cc, RW_WIDTH)) for c in chunks], axis=0)
        d["at"] = (d.pop("a_") * jnp.exp(cum - d.pop("lw") - cmid)).astype(BF16)
        d["rt"] = (d["r"] * jnp.exp(cum - cmid)).astype(BF16)
        e_neg = jnp.exp(cmid - cum)
        d["bt"] = (d.pop("b_") * e_neg).astype(BF16)
        d["kt"] = (d.pop("k_") * e_neg).astype(BF16)
        d["vb"] = d["vx"].astype(BF16)

    def token_stages(chunks):
        return [functools.partial(f, chunks) for f in (tw_project, tw_decay, tw_scale)]

    lane = lax.broadcasted_iota(jnp.int32, (cc, LANE), 1)
    zero = jnp.zeros((cc, LANE), BF16)

    def stack(x):
        return jnp.concatenate([jnp.where(lane < HEAD_DIM, x, zero), jnp.where(lane >= HEAD_DIM, x, zero)], axis=0)

    n2 = 2 * cc
    n_pairs = RW_HEADS // 2
    pairs = range(n_pairs)
    eye = (lax.broadcasted_iota(jnp.int32, (n2, n2), 0) == lax.broadcasted_iota(jnp.int32, (n2, n2), 1)).astype(F32)
    dat = {(c, p): {} for c in range(n_chunks) for p in pairs}

    def group_of(c):
        return first[0] if c in first else second[0]

    def st_scores(cps):
        for c, p in cps:
            tk = tok[group_of(c)]
            off = (c - group_of(c)) * cc
            rs, lanes = slice(off, off + cc), slice(p * LANE, (p + 1) * LANE)
            d = dat[c, p]
            d["a"], d["r"], d["b"], d["k"], d["v"] = [
                stack(tk[name][rs, lanes]) for name in ("at", "rt", "bt", "kt", "vb")]
        for cp in cps:
            d = dat[cp]
            d["m"] = _dot_nt(jnp.concatenate([d["a"], d["r"]], axis=0), jnp.concatenate([d["b"], d["k"]], axis=0))

    def st_local(cps):
        for cp in cps:
            d = dat[cp]
            m = d.pop("m")
            d["l"] = jnp.where(strict, m[:n2, :n2], 0.0)
            d["mrb"] = jnp.where(incl, m[n2:, :n2], 0.0).astype(BF16)
            mk = jnp.concatenate([jnp.where(strict, m[:n2, n2:], 0.0), jnp.where(incl, m[n2:, n2:], 0.0)], axis=0)
            mv = _dot(mk.astype(BF16), d["v"])
            d["mv"], d["y0"] = mv[:n2].astype(BF16), mv[n2:]
        for cp in cps:
            d = dat[cp]
            d["vk"] = _dot_tn(d["v"], d["k"])
        for cp in cps:
            d = dat[cp]
            lb = d["l"].astype(BF16)
            d["t"] = eye + d["l"]
            d["l"] = _dot(lb, lb)

    def st_double(cps, last):
        for cp in cps:
            d = dat[cp]
            lb, tb = d["l"].astype(BF16), d["t"].astype(BF16)
            if last:
                d["t"] = d["t"] + _dot(lb, tb)
            else:
                z = _dot(lb, jnp.concatenate([lb, tb], axis=1))
                d["l"], d["t"] = z[:, :n2], d["t"] + z[:, n2:]

    def st_solve(cps):
        for cp in cps:
            d = dat[cp]
            wu = _dot(d["t"].astype(BF16), jnp.concatenate([d["a"], d["mv"]], axis=1))
            d["wr"] = jnp.concatenate([wu[:, :LANE].astype(BF16), d["r"]], axis=0)
            d["u0"] = wu[:, LANE:]
            del d["l"], d["t"], d["a"], d["mv"]

    def local_stages(cps):
        n_double = int(math.log2(cc)) - 1
        return ([lambda: st_scores(cps), lambda: st_local(cps)]
                + [functools.partial(st_double, cps, s_i + 1 == n_double) for s_i in range(n_double)]
                + [lambda: st_solve(cps)])

    state = [s_ref[p] for p in pairs]
    ychunks = {}

    def ch_enter(c):
        g_mid = jnp.exp(cmids[c])
        for p in pairs:
            d = dat[c, p]
            d["sp"] = state[p] * g_mid[:, p * LANE:(p + 1) * LANE]
            d["ws"] = _dot_nt(d["wr"], d["sp"].astype(BF16))

    def ch_leave(c):
        off = (c - group_of(c)) * cc
        g_end = jnp.exp(tok[group_of(c)]["cum"][off + cc - 1:off + cc, :] - cmids[c])
        y2 = []
        for p in pairs:
            d = dat[c, p]
            d["ub"] = (d["ws"][:n2] + d["u0"]).astype(BF16)
        for p in pairs:
            d = dat[c, p]
            y2.append(d["ws"][n2:] + _dot(d["mrb"], d["ub"]) + d["y0"])
            state[p] = (d["sp"] + _dot_tn(d["ub"], d["b"]) + d["vk"]) * g_end[:, p * LANE:(p + 1) * LANE]
            dat[c, p] = None
        ychunks[c] = jnp.concatenate([t[:cc] + t[cc:] for t in y2], axis=1)

    def finish(chunks):
        tk = tok[chunks[0]]
        y = jnp.concatenate([ychunks[c] for c in chunks], axis=0) if len(chunks) > 1 else ychunks[chunks[0]]
        ym = _seg_sum(y, e_ref) * (1.0 / HEAD_DIM)
        yc = y - ym
        yv = _seg_sum(yc * yc, e_ref) * (1.0 / HEAD_DIM)
        yn = yc * lax.rsqrt(yv + GN_EPS) * lng_ref[...] + lnb_ref[...]
        bonus = _seg_sum(tk["r"] * tk["kmod"] * rk_ref[...], e_ref) * tk["vx"]
        y_ref[0, chunks[0] * cc:(chunks[-1] + 1) * cc, :] = ((yn + bonus) * tk["g"]).astype(BF16)

    def emit(main, side):
        for stage in main:
            stage()
            if side:
                side.pop(0)()
        while side:
            side.pop(0)()

    for stage in token_stages(first):
        stage()
    emit(local_stages([(c, p) for c in first for p in pairs]), token_stages(second) if second else [])
    serial = [f for c in first for f in (functools.partial(ch_enter, c), functools.partial(ch_leave, c))]
    emit(local_stages([(c, p) for c in second for p in pairs]) if second else [], serial)
    if second:
        ch_enter(second[0])
        finish(first)
        ch_leave(second[0])
        for c in second[1:]:
            ch_enter(c)
            ch_leave(c)
        finish(second)
    else:
        finish(first)
    for p in pairs:
        s_ref[p] = state[p]

    @pl.when(t == pl.num_programs(1) - 1)
    def _():
        sfin_ref[0] = s_ref[...]


def _rwkv(pr, prev, s0, w, t_valid):
    batch, seq, _ = pr.shape
    rows = min(RW_ROWS, seq)
    n_pairs = RW_HEADS // 2
    vec = lambda n: _const_spec((1, n))
    return pl.pallas_call(
        functools.partial(_rwkv_kernel, n_chunks=rows // RW_CHUNK, t_valid=t_valid),
        grid=(batch, seq // rows),
        in_specs=[pl.BlockSpec((1, rows, RW_COLS), lambda b, t: (b, t, 0)),
                  pl.BlockSpec((1, 1, RW_COLS), lambda b, t: (b, 0, 0)),
                  pl.BlockSpec((1, n_pairs, LANE, LANE), lambda b, t: (b, 0, 0, 0)),
                  vec(RW_COLS), vec(RW_WIDTH), _const_spec((LANE, RW_WIDTH)), _const_spec((LANE, RW_WIDTH)),
                  vec(RW_WIDTH), _const_spec((LANE, RW_WIDTH)), _const_spec((LANE, RW_WIDTH)),
                  vec(RW_WIDTH), vec(RW_WIDTH), vec(RW_WIDTH), vec(RW_WIDTH), vec(RW_WIDTH),
                  _const_spec((LANE, LANE))],
        out_specs=[pl.BlockSpec((1, rows, RW_WIDTH), lambda b, t: (b, t, 0)),
                   pl.BlockSpec((1, n_pairs, LANE, LANE), lambda b, t: (b, 0, 0, 0))],
        out_shape=[jax.ShapeDtypeStruct((batch, seq, RW_WIDTH), BF16),
                   jax.ShapeDtypeStruct((batch, n_pairs, LANE, LANE), F32)],
        scratch_shapes=[pltpu.VMEM((n_pairs, LANE, LANE), F32), pltpu.VMEM((1, RW_COLS), F32)],
        compiler_params=_params(("parallel", "arbitrary")),
        name="rwkv7",
    )(pr, prev, s0, w["mu"], w["w0"], w["w2h"], w["w2l"], w["a0"], w["a2"], w["g2"],
      w["kk"], w["ka"], w["rk"], w["ln_g"], w["ln_b"], w["seg"])


def _even_out_kernel(x_ref, of_ref, yr_ref, wo_ref, g3_ref, gpre_ref, gpost_ref, win_ref, wout_ref,
                     o_ref, act_ref):
    mixed = _dot(of_ref[...], wo_ref[:FOX_WIDTH, :]) + _dot(yr_ref[...], wo_ref[FOX_WIDTH:, :])
    x = x_ref[...] + _rms(mixed, g3_ref[...])
    o_ref[...] = _ffn_apply(x, gpre_ref[...], gpost_ref[...], win_ref, wout_ref, act_ref)


def _even_out(x, of, yr, wo, g3, gpre, gpost, win, wout, widx):
    n = x.shape[0]
    tm = min(ROW_TILE, n)
    vec = _const_spec((1, D_MODEL))
    return pl.pallas_call(
        _even_out_kernel,
        grid=(n // tm,),
        in_specs=[_row_spec(tm, D_MODEL), _row_spec(tm, FOX_WIDTH), _row_spec(tm, RW_WIDTH),
                  _const_spec((D_MODEL, D_MODEL)), vec, vec, vec] + _ffn_weight_specs(widx),
        out_specs=_row_spec(tm, D_MODEL),
        out_shape=jax.ShapeDtypeStruct((n, D_MODEL), F32),
        scratch_shapes=[pltpu.VMEM((tm, D_FF), BF16)],
        compiler_params=_params(("parallel",)),
        name="even_out_ffn",
    )(x, of, yr, wo, g3, gpre, gpost, win, wout)


def _gelu(x):
    return 0.5 * x * (1.0 + lax.erf(x * (2.0 ** -0.5)))


def _odd_kernel(x_ref, g2_ref, win_ref, lng_ref, lnb_ref, wm_ref, bs_ref, wo_ref, g3_ref,
                gpre_ref, gpost_ref, fwin_ref, fwout_ref, o_ref, *rest):
    gv_ref = rest[0] if len(rest) == 3 else None
    act_ref, gate_ref = rest[-2:]
    x = x_ref[...]
    tm = x.shape[0]
    h = _rms(x, g2_ref[...]).astype(BF16)
    v = _gelu(_dot(h, win_ref[:, GM_WIDTH:]))
    vm = jnp.mean(v, axis=-1, keepdims=True)
    vc = v - vm
    var = jnp.mean(vc * vc, axis=-1, keepdims=True)
    v = vc * lax.rsqrt(var + LN_EPS) * lng_ref[...] + lnb_ref[...]
    if gv_ref is not None:
        gv_ref[...] = v
    vb = v.astype(BF16)
    u = _gelu(_dot(h, win_ref[:, :GM_WIDTH]))
    for c in range(tm // GM_CHUNK):
        rows = slice(c * GM_CHUNK, (c + 1) * GM_CHUNK)
        sp = jnp.concatenate(
            [_dot(wm_ref[g], vb[rows, g * LANE:(g + 1) * LANE]) for g in range(GM_GROUPS)], axis=1)
        gate_ref[rows, :] = (u[rows, :] * (sp + bs_ref[...])).astype(BF16)
    mixed = _dot(gate_ref[...], wo_ref[...])
    x = x + _rms(mixed, g3_ref[...])
    o_ref[...] = _ffn_apply(x, gpre_ref[...], gpost_ref[...], fwin_ref, fwout_ref, act_ref)


def _odd(x, g2, win, lng, lnb, wm, bs, wo, g3, gpre, gpost, fwin, fwout, widx, emit_v):
    n = x.shape[0]
    tm = min(ROW_TILE, n)
    vec = _const_spec((1, D_MODEL))
    return pl.pallas_call(
        _odd_kernel,
        grid=(n // tm,),
        in_specs=[_row_spec(tm, D_MODEL), vec, _const_spec((D_MODEL, 2 * GM_WIDTH)), vec, vec,
                  _const_spec((GM_GROUPS, GM_CHUNK, GM_CHUNK)), _const_spec((GM_CHUNK, GM_WIDTH)),
                  _const_spec((GM_WIDTH, D_MODEL)), vec, vec, vec] + _ffn_weight_specs(widx),
        out_specs=[_row_spec(tm, D_MODEL)] + [_row_spec(tm, GM_WIDTH)] * emit_v,
        out_shape=[jax.ShapeDtypeStruct((n, D_MODEL), F32)] + [jax.ShapeDtypeStruct((n, GM_WIDTH), F32)] * emit_v,
        scratch_shapes=[pltpu.VMEM((tm, D_FF), BF16), pltpu.VMEM((tm, GM_WIDTH), BF16)],
        compiler_params=_params(("parallel",)),
        name="odd_mixer_ffn",
    )(x, g2, win, lng, lnb, wm, bs, wo, g3, gpre, gpost, fwin, fwout)


def _pair_block_diag(s):
    b = s.shape[0]
    s = s.reshape(b, RW_HEADS // 2, 2, HEAD_DIM, HEAD_DIM)
    z = jnp.zeros_like(s[:, :, 0])
    top = jnp.concatenate([s[:, :, 0], z], axis=-1)
    bot = jnp.concatenate([z, s[:, :, 1]], axis=-1)
    return jnp.concatenate([top, bot], axis=-2)


def _pair_unblock(s):
    b = s.shape[0]
    h0 = s[:, :, :HEAD_DIM, :HEAD_DIM]
    h1 = s[:, :, HEAD_DIM:, HEAD_DIM:]
    return jnp.stack([h0, h1], axis=2).reshape(b, RW_HEADS, HEAD_DIM, HEAD_DIM)


def kernel(x_prompt, x_sample, cache_fox_k, cache_fox_v, cache_fox_logf, state_rwkv, state_rwkv_shift,
           norm_g, ffn_w_in, ffn_w_out, even_w_in, fox_bf, rw_mu, rw_w0, rw_w2, rw_a0, rw_a2, rw_g2,
           rw_kk, rw_ka, rw_rk, rw_ln_g, rw_ln_b, even_w_out, gm_w_in, gm_ln_g, gm_ln_b, gm_w_s,
           gm_b_s, gm_w_out):
    batch, seq, _ = x_prompt.shape
    dbatch, dseq, _ = x_sample.shape
    past = cache_fox_k.shape[2]
    depth = norm_g.shape[0]
    xp = x_prompt.reshape(batch * seq, D_MODEL)
    xs = x_sample.reshape(dbatch * dseq, D_MODEL)
    ng = lambda layer, i: norm_g[layer, i].reshape(1, D_MODEL)
    win = ffn_w_in.astype(BF16).reshape(depth * 2, D_MODEL, 2 * D_FF)
    wout = ffn_w_out.astype(BF16).reshape(depth * 2, D_FF, D_MODEL)

    fk_p, fv_p, fl_p, rs_p, rsh_p = [], [], [], [], []
    fk_s, fv_s, fl_s, rs_s, rsh_s = [], [], [], [], []
    gv_s = []
    for layer in range(depth):
        j = layer // 2
        if layer % 2 == 0:
            xp = _ffn(xp, ng(layer, 0), ng(layer, 1), win, wout, 2 * layer)
            xs = _ffn(xs, ng(layer, 0), ng(layer, 1), win, wout, 2 * layer)
            ew = even_w_in[j]
            wqkv = ew[:, :3 * FOX_WIDTH].astype(BF16)
            wf = jnp.pad(ew[:, 3 * FOX_WIDTH:FOX_COLS], ((0, 0), (0, LANE - FOX_HEADS))).astype(BF16)
            bf = jnp.pad(fox_bf[j], (0, LANE - FOX_HEADS)).reshape(1, LANE)
            wrw = ew[:, FOX_COLS:].astype(BF16)
            w2 = jnp.pad(rw_w2[j], ((0, LANE - rw_w2.shape[1]), (0, 0)))
            w2h = w2.astype(BF16)
            head_of = jnp.arange(LANE) // HEAD_DIM
            rw = dict(
                mu=rw_mu[j].reshape(1, RW_COLS), w0=rw_w0[j].reshape(1, RW_WIDTH),
                w2h=w2h, w2l=(w2 - w2h.astype(F32)).astype(BF16),
                a0=rw_a0[j].reshape(1, RW_WIDTH),
                a2=jnp.pad(rw_a2[j], ((LANE - rw_a2.shape[1], 0), (0, 0))).astype(BF16),
                g2=rw_g2[j].astype(BF16),
                kk=rw_kk[j].reshape(1, RW_WIDTH), ka=rw_ka[j].reshape(1, RW_WIDTH),
                rk=rw_rk[j].reshape(1, RW_WIDTH),
                ln_g=rw_ln_g[j].reshape(1, RW_WIDTH), ln_b=rw_ln_b[j].reshape(1, RW_WIDTH),
                seg=(head_of[:, None] == head_of[None, :]).astype(BF16))
            wo = even_w_out[j].astype(BF16)

            q, kb, vb, k, v, lf, lfp, pr = _proj(xp, ng(layer, 2), wqkv, wf, bf, wrw)
            of = _fox_prompt(q, kb, vb, lfp, batch, seq)
            pr3 = pr.reshape(batch, seq, RW_COLS)
            yr, sfin = _rwkv(pr3, jnp.zeros((batch, 1, RW_COLS), F32),
                             jnp.zeros((batch, RW_HEADS // 2, LANE, LANE), F32), rw, seq)
            xp = _even_out(xp, of, yr.reshape(batch * seq, RW_WIDTH), wo, ng(layer, 3),
                           ng(layer, 4), ng(layer, 5), win, wout, 2 * layer + 1)
            fk_p.append(k.reshape(batch, seq, FOX_HEADS, HEAD_DIM))
            fv_p.append(v.reshape(batch, seq, FOX_HEADS, HEAD_DIM))
            fl_p.append(lf.reshape(batch, seq, FOX_HEADS))
            rs_p.append(_pair_unblock(sfin))
            rsh_p.append(pr3[:, seq - 1:, :])

            q, kb, vb, k, v, lf, _, pr = _proj(xs, ng(layer, 2), wqkv, wf, bf, wrw)
            lfn_t = jnp.swapaxes(lf.reshape(dbatch, dseq, FOX_HEADS), 1, 2)
            lfc_t = jnp.swapaxes(cache_fox_logf[j], 1, 2)
            of = _fox_sample(q, kb, vb, lfn_t, cache_fox_k[j], cache_fox_v[j], lfc_t, dbatch, dseq, past)
            pr3 = pr.reshape(dbatch, dseq, RW_COLS)
            pad = (-dseq) % RW_CHUNK
            yr, sfin = _rwkv(jnp.pad(pr3, ((0, 0), (0, pad), (0, 0))), state_rwkv_shift[j],
                             _pair_block_diag(state_rwkv[j]), rw, dseq)
            xs = _even_out(xs, of, yr[:, :dseq].reshape(dbatch * dseq, RW_WIDTH), wo, ng(layer, 3),
                           ng(layer, 4), ng(layer, 5), win, wout, 2 * layer + 1)
            fk_s.append(k.reshape(dbatch, dseq, FOX_HEADS, HEAD_DIM))
            fv_s.append(v.reshape(dbatch, dseq, FOX_HEADS, HEAD_DIM))
            fl_s.append(lf.reshape(dbatch, dseq, FOX_HEADS))
            rs_s.append(_pair_unblock(sfin))
            rsh_s.append(pr3[:, dseq - 1:, :])
        else:
            xp = _ffn(xp, ng(layer, 0), ng(layer, 1), win, wout, 2 * layer)
            xs = _ffn(xs, ng(layer, 0), ng(layer, 1), win, wout, 2 * layer)
            cpos = jnp.arange(GM_CHUNK) // CHUNK
            wm = jnp.where(cpos[None, :] <= cpos[:, None], gm_w_s[j], 0.0)
            bs = jnp.repeat(gm_b_s[j].T, GM_WIDTH // GM_GROUPS, axis=1)
            reps = GM_CHUNK // dseq
            wm_s = jnp.einsum('ab,gts->gatbs', jnp.eye(reps, dtype=F32), wm[:, :dseq, :dseq])
            wm_s = wm_s.reshape(GM_GROUPS, GM_CHUNK, GM_CHUNK)
            bs_s = jnp.tile(bs[:dseq], (reps, 1))
            odd = (ng(layer, 2), gm_w_in[j].astype(BF16), gm_ln_g[j].reshape(1, GM_WIDTH),
                   gm_ln_b[j].reshape(1, GM_WIDTH))
            tail = (gm_w_out[j].astype(BF16), ng(layer, 3), ng(layer, 4), ng(layer, 5),
                    win, wout, 2 * layer + 1)
            xp, = _odd(xp, *odd, wm.astype(BF16), bs, *tail, emit_v=False)
            xs, gv = _odd(xs, *odd, wm_s.astype(BF16), bs_s, *tail, emit_v=True)
            gv_s.append(gv.reshape(dbatch, dseq, GM_WIDTH))
    return (xp.reshape(batch, seq, D_MODEL), xs.reshape(dbatch, dseq, D_MODEL),
            jnp.stack(fk_p), jnp.stack(fv_p), jnp.stack(fl_p), jnp.stack(rs_p), jnp.stack(rsh_p),
            jnp.stack(fk_s), jnp.stack(fv_s), jnp.stack(fl_s), jnp.stack(rs_s), jnp.stack(rsh_s),
            jnp.stack(gv_s))
```

```python
import functools
import math

import jax
import jax.numpy as jnp
from jax import lax
from jax.experimental import pallas as pl
from jax.experimental.pallas import tpu as pltpu

F32 = jnp.float32
BF16 = jnp.bfloat16

D_MODEL = 1024
D_FF = 2816
HEAD_DIM = 64
FOX_WIDTH = 512
FOX_HEADS = 8
RW_WIDTH = 512
RW_HEADS = 8
RW_COLS = 1792
RW_OFF_W = 1536
RW_OFF_G = 1664
FOX_COLS = 3 * FOX_WIDTH + FOX_HEADS
GM_WIDTH = 1024
GM_CHUNK = 128
GM_GROUPS = 8
CHUNK = 64
FFN_RES_SCALE = 0.5
RMS_EPS = 1e-6
LN_EPS = 1e-5
GN_EPS = 64e-5

LANE = 128
ROW_TILE = 512
FF_CHUNK = 256
FOX_BLOCK = 256
FOX_VROWS = HEAD_DIM + 16
LOG2E = 1.4426950408889634
RW_CHUNK = 64
RW_ROWS = 512
RW_GROUP = 2
NEG_BIG = -1e30
VMEM_LIMIT = 56 * 1024 * 1024


def _params(sem):
    return pltpu.CompilerParams(dimension_semantics=sem, vmem_limit_bytes=VMEM_LIMIT)


def _const_spec(shape):
    n = len(shape)
    return pl.BlockSpec(shape, lambda *_: (0,) * n, pipeline_mode=pl.Buffered(1))


def _dot(a, b):
    return jnp.dot(a, b, preferred_element_type=F32)


def _dot_nt(a, b):
    return lax.dot_general(a, b, (((1,), (1,)), ((), ())), preferred_element_type=F32)


def _dot_tn(a, b):
    return lax.dot_general(a, b, (((0,), (0,)), ((), ())), preferred_element_type=F32)


def _split2(x):
    hi = x.astype(BF16)
    lo = (x - hi.astype(F32)).astype(BF16)
    return hi, lo


def _split3(x):
    hi = x.astype(BF16)
    r1 = x - hi.astype(F32)
    mid = r1.astype(BF16)
    lo = (r1 - mid.astype(F32)).astype(BF16)
    return hi, mid, lo


def _rms(x, g):
    return x * lax.rsqrt(jnp.mean(x * x, axis=-1, keepdims=True) + RMS_EPS) * g


def _sigmoid(x):
    return 1.0 / (1.0 + jnp.exp(-x))


def _log_sigmoid(z):
    return jnp.minimum(z, 0.0) - jnp.log1p(jnp.exp(-jnp.abs(z)))


def _ffn_apply(x, gpre, gpost, win_ref, wout_ref, act_ref):
    h = _rms(x, gpre).astype(BF16)
    for c in range(D_FF // FF_CHUNK):
        lo = c * FF_CHUNK
        gate = _dot(h, win_ref[:, lo:lo + FF_CHUNK])
        up = _dot(h, win_ref[:, D_FF + lo:D_FF + lo + FF_CHUNK])
        act_ref[:, lo:lo + FF_CHUNK] = (gate * _sigmoid(gate) * up).astype(BF16)
    y = _dot(act_ref[...], wout_ref[...])
    return x + FFN_RES_SCALE * _rms(y, gpost)


def _ffn_kernel(x_ref, gpre_ref, gpost_ref, win_ref, wout_ref, o_ref, act_ref):
    o_ref[...] = _ffn_apply(x_ref[...], gpre_ref[...], gpost_ref[...], win_ref, wout_ref, act_ref)


def _row_spec(tm, width):
    return pl.BlockSpec((tm, width), lambda i: (i, 0))


def _ffn_weight_specs(widx):
    pick = lambda *_: (widx, 0, 0)
    return [pl.BlockSpec((None, D_MODEL, 2 * D_FF), pick, pipeline_mode=pl.Buffered(1)),
            pl.BlockSpec((None, D_FF, D_MODEL), pick, pipeline_mode=pl.Buffered(1))]


def _ffn(x, gpre, gpost, win, wout, widx):
    n = x.shape[0]
    tm = min(ROW_TILE, n)
    return pl.pallas_call(
        _ffn_kernel,
        grid=(n // tm,),
        in_specs=[_row_spec(tm, D_MODEL), _const_spec((1, D_MODEL)), _const_spec((1, D_MODEL))]
        + _ffn_weight_specs(widx),
        out_specs=_row_spec(tm, D_MODEL),
        out_shape=jax.ShapeDtypeStruct((n, D_MODEL), F32),
        scratch_shapes=[pltpu.VMEM((tm, D_FF), BF16)],
        compiler_params=_params(("parallel",)),
        name="ffn",
    )(x, gpre, gpost, win, wout)


def _head_major(x):
    per_head = []
    for g in range(x.shape[1] // LANE):
        xg = x[:, g * LANE:(g + 1) * LANE]
        per_head += [xg, pltpu.roll(xg, HEAD_DIM, 1)]
    return pltpu.einshape("htd->thd", jnp.stack(per_head, axis=0))[:, :, :HEAD_DIM]


def _proj_kernel(x_ref, g_ref, wqkv_ref, wf_ref, bf_ref, wrw_ref,
                 q_ref, kb_ref, vb_ref, k_ref, v_ref, lf_ref, lfp_ref, pr_ref):
    h = _rms(x_ref[...], g_ref[...]).astype(BF16)
    q_ref[...] = _dot(h, wqkv_ref[:, :FOX_WIDTH]).astype(BF16)
    k = _dot(h, wqkv_ref[:, FOX_WIDTH:2 * FOX_WIDTH])
    v = _dot(h, wqkv_ref[:, 2 * FOX_WIDTH:])
    kb_ref[...] = k.astype(BF16)
    vb_ref[...] = v.astype(BF16)
    k_ref[...] = _head_major(k)
    v_ref[...] = _head_major(v)
    lf = _log_sigmoid(_dot(h, wf_ref[...]) + bf_ref[...])
    lf_ref[...] = lf[:, :FOX_HEADS]
    lfp_ref[...] = lf
    pr_ref[...] = _dot(h, wrw_ref[...])


def _proj(x, g, wqkv, wf, bf, wrw):
    n = x.shape[0]
    tm = min(ROW_TILE, n)
    return pl.pallas_call(
        _proj_kernel,
        grid=(n // tm,),
        in_specs=[_row_spec(tm, D_MODEL), _const_spec((1, D_MODEL)),
                  _const_spec((D_MODEL, 3 * FOX_WIDTH)), _const_spec((D_MODEL, LANE)),
                  _const_spec((1, LANE)), _const_spec((D_MODEL, RW_COLS))],
        out_specs=[_row_spec(tm, FOX_WIDTH), _row_spec(tm, FOX_WIDTH), _row_spec(tm, FOX_WIDTH),
                   pl.BlockSpec((tm, FOX_HEADS, HEAD_DIM), lambda i: (i, 0, 0)),
                   pl.BlockSpec((tm, FOX_HEADS, HEAD_DIM), lambda i: (i, 0, 0)),
                   _row_spec(tm, FOX_HEADS), _row_spec(tm, LANE), _row_spec(tm, RW_COLS)],
        out_shape=[jax.ShapeDtypeStruct((n, FOX_WIDTH), BF16),
                   jax.ShapeDtypeStruct((n, FOX_WIDTH), BF16),
                   jax.ShapeDtypeStruct((n, FOX_WIDTH), BF16),
                   jax.ShapeDtypeStruct((n, FOX_HEADS, HEAD_DIM), F32),
                   jax.ShapeDtypeStruct((n, FOX_HEADS, HEAD_DIM), F32),
                   jax.ShapeDtypeStruct((n, FOX_HEADS), F32),
                   jax.ShapeDtypeStruct((n, LANE), F32),
                   jax.ShapeDtypeStruct((n, RW_COLS), F32)],
        compiler_params=_params(("parallel",)),
        name="even_proj",
    )(x, g, wqkv, wf, bf, wrw)


def _lane_cumsum(rows, carry):
    r = lax.broadcasted_iota(jnp.int32, (LANE, LANE), 0)
    c = lax.broadcasted_iota(jnp.int32, (LANE, LANE), 1)
    upper = (r <= c).astype(BF16)
    hi, mid, lo = _split3(rows)
    return _dot(hi, upper) + _dot(mid, upper) + _dot(lo, upper) + carry


def _fox_prompt_kernel(q_ref, k_ref, v_ref, lfp_ref, o_ref, kx_ref, vt_ref, qx_ref, m_ref, alpha_ref, p_ref,
                       acc_ref, *, seq):
    i = pl.program_id(1)
    bq = FOX_BLOCK
    n_pairs = FOX_HEADS // 2
    heads = range(FOX_HEADS)
    val_rows = [slice(h * FOX_VROWS, h * FOX_VROWS + HEAD_DIM) for h in heads]
    ext_rows = [slice(h * FOX_VROWS, (h + 1) * FOX_VROWS) for h in heads]

    @pl.when(i == 0)
    def _():
        r = lax.broadcasted_iota(jnp.int32, (LANE, LANE), 0)
        c = lax.broadcasted_iota(jnp.int32, (LANE, LANE), 1)
        tri = (c <= r).astype(BF16)
        ones_row = (r[:FOX_VROWS - HEAD_DIM] == 0).astype(BF16)
        carry = jnp.zeros((1, LANE), F32)
        for blk in range(seq // LANE):
            rows = slice(blk * LANE, (blk + 1) * LANE)
            hi, mid, lo = _split3(jnp.where(c < FOX_HEADS, lfp_ref[rows, :], 0.0))
            cum = _dot(tri, hi) + _dot(tri, mid) + _dot(tri, lo) + carry
            carry = cum[LANE - 1:LANE, :]
            chi, cmid, clo = _split3(cum * LOG2E)
            feat = (chi.astype(F32) + pltpu.roll(cmid.astype(F32), FOX_HEADS, 1)
                    + pltpu.roll(clo.astype(F32), 2 * FOX_HEADS, 1)).astype(BF16)
            for p in range(n_pairs):
                lanes = slice(p * LANE, (p + 1) * LANE)
                kx_ref[p, rows, :LANE] = k_ref[rows, lanes]
                kx_ref[p, rows, LANE:] = feat
                vt = v_ref[rows, lanes].astype(F32).T.astype(BF16)
                vt_ref[val_rows[2 * p], rows] = vt[:HEAD_DIM]
                vt_ref[val_rows[2 * p + 1], rows] = vt[HEAD_DIM:]
            for h in heads:
                vt_ref[h * FOX_VROWS + HEAD_DIM:(h + 1) * FOX_VROWS, rows] = ones_row

    q0 = pl.multiple_of(i * bq, bq)
    lane = lax.broadcasted_iota(jnp.int32, (1, LANE), 1)
    krow = lax.broadcasted_iota(jnp.int32, (bq, bq), 0)
    qcol = lax.broadcasted_iota(jnp.int32, (bq, bq), 1)
    causal = krow <= qcol
    for p in range(n_pairs):
        qp = q_ref[:, p * LANE:(p + 1) * LANE].astype(F32) * (HEAD_DIM ** -0.5 * LOG2E)
        for hh in range(2):
            h = 2 * p + hh
            in_head = (lane >= HEAD_DIM) if hh else (lane < HEAD_DIM)
            pick = (lane == h) | (lane == h + FOX_HEADS) | (lane == h + 2 * FOX_HEADS)
            sel = jnp.where(pick, -1.0, 0.0).astype(BF16)
            qx_ref[h, :, :LANE] = jnp.where(in_head, qp, 0.0).astype(BF16)
            qx_ref[h, :, LANE:] = jnp.broadcast_to(sel, (bq, LANE))

    def scores(j0):
        return [_dot_nt(kx_ref[h // 2, pl.ds(j0, bq), :], qx_ref[h]) for h in heads]

    def softmax(sts, diag):
        for h in heads:
            st = jnp.where(causal, sts[h], NEG_BIG) if diag else sts[h]
            m_old = m_ref[h:h + 1, :]
            m_new = jnp.maximum(m_old, jnp.max(st, axis=0, keepdims=True))
            alpha_ref[h:h + 1, :] = jnp.exp2(m_old - m_new)
            m_ref[h:h + 1, :] = m_new
            p_ref[h] = jnp.exp2(st - m_new).astype(BF16)

    def accumulate(j0):
        pvs = [_dot(vt_ref[ext_rows[h], pl.ds(j0, bq)], p_ref[h]) for h in heads]
        for h in heads:
            acc_ref[ext_rows[h], :] = alpha_ref[h:h + 1, :] * acc_ref[ext_rows[h], :] + pvs[h]

    m_ref[...] = jnp.full(m_ref.shape, NEG_BIG, F32)
    acc_ref[...] = jnp.zeros(acc_ref.shape, F32)
    softmax(scores(q0), True)

    def body(j, _):
        j0 = pl.multiple_of(j * bq, bq)
        sts = scores(j0)
        accumulate(pl.multiple_of(jnp.where(j == 0, q0, j0 - bq), bq))
        softmax(sts, False)
        return 0

    lax.fori_loop(0, i, body, 0)
    accumulate(pl.multiple_of(jnp.where(i == 0, q0, q0 - bq), bq))
    out = [acc_ref[val_rows[h], :] * (1.0 / acc_ref[h * FOX_VROWS + HEAD_DIM:h * FOX_VROWS + HEAD_DIM + 1, :])
           for h in heads]
    o_ref[...] = jnp.concatenate(out, axis=0).T.astype(BF16)


def _fox_prompt(q, k, v, lfp, batch, seq):
    nq = seq // FOX_BLOCK
    return pl.pallas_call(
        functools.partial(_fox_prompt_kernel, seq=seq),
        grid=(batch, nq),
        in_specs=[pl.BlockSpec((FOX_BLOCK, FOX_WIDTH), lambda b, i: (b * nq + i, 0)),
                  pl.BlockSpec((seq, FOX_WIDTH), lambda b, i: (b, 0)),
                  pl.BlockSpec((seq, FOX_WIDTH), lambda b, i: (b, 0)),
                  pl.BlockSpec((seq, LANE), lambda b, i: (b, 0))],
        out_specs=pl.BlockSpec((FOX_BLOCK, FOX_WIDTH), lambda b, i: (b * nq + i, 0)),
        out_shape=jax.ShapeDtypeStruct((batch * seq, FOX_WIDTH), BF16),
        scratch_shapes=[pltpu.VMEM((FOX_HEADS // 2, seq, 2 * LANE), BF16),
                        pltpu.VMEM((FOX_HEADS * FOX_VROWS, seq), BF16),
                        pltpu.VMEM((FOX_HEADS, FOX_BLOCK, 2 * LANE), BF16),
                        pltpu.VMEM((FOX_HEADS, FOX_BLOCK), F32),
                        pltpu.VMEM((FOX_HEADS, FOX_BLOCK), F32),
                        pltpu.VMEM((FOX_HEADS, FOX_BLOCK, FOX_BLOCK), BF16),
                        pltpu.VMEM((FOX_HEADS * FOX_VROWS, FOX_BLOCK), F32)],
        compiler_params=_params(("parallel", "arbitrary")),
        name="fox_prompt",
    )(q, k, v, lfp)


def _fox_sample_kernel(q_ref, kn_ref, vn_ref, lfn_ref, kc_ref, vc_ref, lfc_ref, o_ref,
                       kb_ref, vb_ref, *, past, tnew):
    heads = range(FOX_HEADS)
    hrows = [slice(h * HEAD_DIM, (h + 1) * HEAD_DIM) for h in heads]
    for h in heads:
        kb_ref[hrows[h], :] = kc_ref[0, h].astype(BF16)
        vb_ref[hrows[h], :] = vc_ref[0, h].astype(BF16)
    q = q_ref[...].astype(F32) * (HEAD_DIM ** -0.5)
    qh = [q[:, hrows[h]].astype(BF16) for h in heads]

    carry = jnp.zeros((FOX_HEADS, 1), F32)
    cts = []
    for b in range(past // LANE):
        c = _lane_cumsum(lfc_ref[0, :, b * LANE:(b + 1) * LANE], carry)
        cts.append(c)
        carry = c[:, LANE - 1:LANE]
    total = carry
    r = lax.broadcasted_iota(jnp.int32, (tnew, tnew), 0)
    c = lax.broadcasted_iota(jnp.int32, (tnew, tnew), 1)
    hi, mid, lo = _split3(lfn_ref[0])
    upper = (r <= c).astype(BF16)
    cnew = _dot(hi, upper) + _dot(mid, upper) + _dot(lo, upper)

    ccache = jnp.concatenate(cts, axis=1) - total
    kn, vn = kn_ref[...], vn_ref[...]
    sc = [_dot(qh[h], kb_ref[hrows[h], :]) - ccache[h:h + 1, :] for h in heads]
    sn = [jnp.where(c <= r, _dot_nt(qh[h], kn[:, hrows[h]]) - cnew[h:h + 1, :], NEG_BIG) for h in heads]
    pcs, pns, ls = [], [], []
    for h in heads:
        m = jnp.maximum(jnp.max(sc[h], axis=1, keepdims=True), jnp.max(sn[h], axis=1, keepdims=True))
        pc = jnp.exp(sc[h] - m)
        pn = jnp.exp(sn[h] - m)
        ls.append(jnp.sum(pc, axis=1, keepdims=True) + jnp.sum(pn, axis=1, keepdims=True))
        pcs.append(pc.astype(BF16))
        pns.append(pn.astype(BF16))
    outs = [(_dot_nt(pcs[h], vb_ref[hrows[h], :]) + _dot(pns[h], vn[:, hrows[h]])) / ls[h] for h in heads]
    o_ref[...] = jnp.concatenate(outs, axis=1).astype(BF16)


def _fox_sample(q, kn, vn, lfn_t, kc, vc, lfc_t, batch, tnew, past):
    return pl.pallas_call(
        functools.partial(_fox_sample_kernel, past=past, tnew=tnew),
        grid=(batch,),
        in_specs=[pl.BlockSpec((tnew, FOX_WIDTH), lambda b: (b, 0)),
                  pl.BlockSpec((tnew, FOX_WIDTH), lambda b: (b, 0)),
                  pl.BlockSpec((tnew, FOX_WIDTH), lambda b: (b, 0)),
                  pl.BlockSpec((1, FOX_HEADS, tnew), lambda b: (b, 0, 0)),
                  pl.BlockSpec((1, FOX_HEADS, HEAD_DIM, past), lambda b: (b, 0, 0, 0)),
                  pl.BlockSpec((1, FOX_HEADS, HEAD_DIM, past), lambda b: (b, 0, 0, 0)),
                  pl.BlockSpec((1, FOX_HEADS, past), lambda b: (b, 0, 0))],
        out_specs=pl.BlockSpec((tnew, FOX_WIDTH), lambda b: (b, 0)),
        out_shape=jax.ShapeDtypeStruct((batch * tnew, FOX_WIDTH), BF16),
        scratch_shapes=[pltpu.VMEM((FOX_WIDTH, past), BF16), pltpu.VMEM((FOX_WIDTH, past), BF16)],
        compiler_params=_params(("parallel",)),
        name="fox_sample",
    )(q, kn, vn, lfn_t, kc, vc, lfc_t)


def _seg_sum(z, e_ref):
    zb = z.astype(BF16)
    return jnp.concatenate(
        [_dot(zb[:, p * LANE:(p + 1) * LANE], e_ref[...]) for p in range(z.shape[1] // LANE)], axis=1)


def _rwkv_kernel(pr_ref, prev_ref, s0_ref, mu_ref, w0_ref, w2h_ref, w2l_ref, a0_ref, a2_ref, g2_ref,
                 kk_ref, ka_ref, rk_ref, lng_ref, lnb_ref, e_ref,
                 y_ref, sfin_ref, s_ref, pv_ref, *, n_chunks, t_valid):
    cc = RW_CHUNK
    t = pl.program_id(1)

    @pl.when(t == 0)
    def _():
        s_ref[...] = s0_ref[0]
        pv_ref[...] = prev_ref[0]

    rows = n_chunks * cc
    gsize = min(RW_GROUP, n_chunks)
    groups = [list(range(c0, c0 + gsize)) for c0 in range(0, n_chunks, gsize)]
    gr = gsize * cc
    ri = lax.broadcasted_iota(jnp.int32, (gr, gr), 0)
    ci = lax.broadcasted_iota(jnp.int32, (gr, gr), 1)
    tri = ((ci <= ri) & (ci // cc == ri // cc)).astype(BF16)
    r2 = lax.broadcasted_iota(jnp.int32, (2 * cc, 2 * cc), 0) % cc
    c2 = lax.broadcasted_iota(jnp.int32, (2 * cc, 2 * cc), 1) % cc
    strict = c2 < r2
    incl = c2 <= r2
    row1 = lax.broadcasted_iota(jnp.int32, (rows, 1), 0)

    pr = pr_ref[0]
    n_real = pr.shape[0]
    if n_real < rows:
        pr = jnp.concatenate([pr, jnp.zeros((rows - n_real, RW_COLS), F32)], axis=0)
    prev_seq = jnp.where(row1 == 0, pv_ref[...], pltpu.roll(pr, 1, 0))
    pv_ref[...] = pr[n_real - 1:n_real, :]
    xm = pr + (prev_seq - pr) * mu_ref[...]
    tok = {}
    cmids = {}

    def tw_project(chunks):
        x = xm[chunks[0] * cc:(chunks[-1] + 1) * cc]
        d = tok[chunks[0]] = dict(r=x[:, :RW_WIDTH], kx=x[:, RW_WIDTH:2 * RW_WIDTH], vx=x[:, 2 * RW_WIDTH:RW_OFF_W])
        xwa = x[:, RW_OFF_W:RW_OFF_G]
        th, tl = _split2(jnp.tanh(xwa))
        d["zw"] = w0_ref[...] + (_dot(th, w2h_ref[...]) + _dot(th, w2l_ref[...]) + _dot(tl, w2h_ref[...]))
        d["za"] = a0_ref[...] + _dot(xwa.astype(BF16), a2_ref[...])
        d["g"] = _dot(_sigmoid(x[:, RW_OFF_G:]).astype(BF16), g2_ref[...])

    def tw_decay(chunks):
        d = tok[chunks[0]]
        lw = -math.exp(-0.5) * _sigmoid(d.pop("zw"))
        asig = _sigmoid(d.pop("za"))
        kkv = d["kx"] * kk_ref[...]
        kkn = kkv / jnp.maximum(jnp.sqrt(_seg_sum(kkv * kkv, e_ref)), 1e-12)
        d["kmod"] = d.pop("kx") * (1.0 + (asig - 1.0) * ka_ref[...])
        d["a_"], d["b_"], d["k_"] = -kkn, kkn * asig, d["kmod"]
        if t_valid < rows:
            valid = row1[chunks[0] * cc:(chunks[-1] + 1) * cc] < t_valid
            lw = jnp.where(valid, lw, 0.0)
            d["b_"] = jnp.where(valid, d["b_"], 0.0)
            d["k_"] = jnp.where(valid, d["k_"], 0.0)
        hi, mid, lo = _split3(lw)
        d["lw"] = lw
        d["cum"] = _dot(tri, hi) + _dot(tri, mid) + _dot(tri, lo)

    def tw_scale(chunks):
        d = tok[chunks[0]]
        cum = d["cum"]
        for i, c in enumerate(chunks):
            cmids[c] = cum[i * cc + cc // 2 - 1:i * cc + cc // 2, :]
        cmid = jnp.concatenate([jnp.broadcast_to(cmids[c], (cc, RW_WIDTH)) for c in chunks], axis=0)
        d["at"] = (d.pop("a_") * jnp.exp(cum - d.pop("lw") - cmid)).astype(BF16)
        d["rt"] = (d["r"] * jnp.exp(cum - cmid)).astype(BF16)
        e_neg = jnp.exp(cmid - cum)
        d["bt"] = (d.pop("b_") * e_neg).astype(BF16)
        d["kt"] = (d.pop("k_") * e_neg).astype(BF16)
        d["vb"] = d["vx"].astype(BF16)

    def token_stages(chunks):
        return [functools.partial(f, chunks) for f in (tw_project, tw_decay, tw_scale)]

    lane = lax.broadcasted_iota(jnp.int32, (cc, LANE), 1)
    zero = jnp.zeros((cc, LANE), BF16)

    def stack(x):
        return jnp.concatenate([jnp.where(lane < HEAD_DIM, x, zero), jnp.where(lane >= HEAD_DIM, x, zero)], axis=0)

    n2 = 2 * cc
    n_pairs = RW_HEADS // 2
    pairs = range(n_pairs)
    eye = (lax.broadcasted_iota(jnp.int32, (n2, n2), 0) == lax.broadcasted_iota(jnp.int32, (n2, n2), 1)).astype(F32)
    dat = {(c, p): {} for c in range(n_chunks) for p in pairs}

    def group_of(c):
        return c - c % gsize

    def st_scores(cps):
        for c, p in cps:
            tk = tok[group_of(c)]
            off = (c - group_of(c)) * cc
            rs, lanes = slice(off, off + cc), slice(p * LANE, (p + 1) * LANE)
            d = dat[c, p]
            d["a"], d["r"], b, k, d["v"] = [
                stack(tk[name][rs, lanes]) for name in ("at", "rt", "bt", "kt", "vb")]
            d["bk"] = jnp.concatenate([b, k], axis=0)
        for cp in cps:
            d = dat[cp]
            d["m"] = _dot_nt(jnp.concatenate([d["a"], d["r"]], axis=0), d["bk"])

    def st_local(cps):
        for cp in cps:
            d = dat[cp]
            m = d.pop("m")
            d["l"] = jnp.where(strict, m[:n2, :n2], 0.0)
            d["mrb"] = jnp.where(incl, m[n2:, :n2], 0.0).astype(BF16)
            mk = jnp.concatenate([jnp.where(strict, m[:n2, n2:], 0.0), jnp.where(incl, m[n2:, n2:], 0.0)], axis=0)
            mv = _dot(mk.astype(BF16), d["v"])
            d["mv"], d["y0"] = mv[:n2].astype(BF16), mv[n2:]
        for cp in cps:
            d = dat[cp]
            lb = d["l"].astype(BF16)
            d["t"] = eye + d["l"]
            d["l"] = _dot(lb, lb)

    def st_double(cps, last):
        for cp in cps:
            d = dat[cp]
            lb, tb = d["l"].astype(BF16), d["t"].astype(BF16)
            if last:
                d["t"] = d["t"] + _dot(lb, tb)
            else:
                z = _dot(lb, jnp.concatenate([lb, tb], axis=1))
                d["l"], d["t"] = z[:, :n2], d["t"] + z[:, n2:]

    def st_solve(cps):
        for cp in cps:
            d = dat[cp]
            wu = _dot(d["t"].astype(BF16), jnp.concatenate([d["a"], d["mv"]], axis=1))
            d["wr"] = jnp.concatenate([wu[:, :LANE].astype(BF16), d["r"]], axis=0)
            d["u0"] = wu[:, LANE:]
            del d["l"], d["t"], d["a"], d["mv"]

    def local_stages(cps):
        n_double = int(math.log2(cc)) - 1
        return ([lambda: st_scores(cps), lambda: st_local(cps)]
                + [functools.partial(st_double, cps, s_i + 1 == n_double) for s_i in range(n_double)]
                + [lambda: st_solve(cps)])

    state = [s_ref[p] for p in pairs]
    ychunks = {}

    def ch_enter(c):
        g_mid = jnp.exp(cmids[c])
        for p in pairs:
            d = dat[c, p]
            d["sp"] = state[p] * g_mid[:, p * LANE:(p + 1) * LANE]
            d["ws"] = _dot_nt(d["wr"], d["sp"].astype(BF16))

    def ch_leave(c):
        off = (c - group_of(c)) * cc
        g_end = jnp.exp(tok[group_of(c)]["cum"][off + cc - 1:off + cc, :] - cmids[c])
        y2 = []
        for p in pairs:
            d = dat[c, p]
            d["ub"] = (d["ws"][:n2] + d["u0"]).astype(BF16)
        for p in pairs:
            d = dat[c, p]
            y2.append(d["ws"][n2:] + _dot(d["mrb"], d["ub"]) + d["y0"])
            upd = _dot_tn(jnp.concatenate([d["ub"], d["v"]], axis=0), d["bk"])
            state[p] = (d["sp"] + upd) * g_end[:, p * LANE:(p + 1) * LANE]
            dat[c, p] = None
        ychunks[c] = jnp.concatenate([t[:cc] + t[cc:] for t in y2], axis=1)

    def finish(chunks):
        tk = tok[chunks[0]]
        y = jnp.concatenate([ychunks[c] for c in chunks], axis=0) if len(chunks) > 1 else ychunks[chunks[0]]
        ym = _seg_sum(y, e_ref) * (1.0 / HEAD_DIM)
        yc = y - ym
        yv = _seg_sum(yc * yc, e_ref) * (1.0 / HEAD_DIM)
        yn = yc * lax.rsqrt(yv + GN_EPS) * lng_ref[...] + lnb_ref[...]
        bonus = _seg_sum(tk["r"] * tk["kmod"] * rk_ref[...], e_ref) * tk["vx"]
        lo = chunks[0] * cc
        hi = min((chunks[-1] + 1) * cc, n_real)
        y_ref[0, lo:hi, :] = ((yn + bonus) * tk["g"]).astype(BF16)[:hi - lo]

    def emit(main, side):
        for stage in main:
            stage()
            if side:
                side.pop(0)()
        while side:
            side.pop(0)()

    def serial_stages(chunks):
        fs = [f for c in chunks for f in (functools.partial(ch_enter, c), functools.partial(ch_leave, c))]
        return fs + [functools.partial(finish, chunks)]

    for stage in token_stages(groups[0]):
        stage()
    for gi, chunks in enumerate(groups):
        nxt = token_stages(groups[gi + 1]) if gi + 1 < len(groups) else []
        prv = serial_stages(groups[gi - 1]) if gi else []
        side = [f for pair in zip(nxt, prv) for f in pair] + nxt[len(prv):] + prv[len(nxt):]
        emit(local_stages([(c, p) for c in chunks for p in pairs]), side)
    for stage in serial_stages(groups[-1]):
        stage()
    for p in pairs:
        s_ref[p] = state[p]

    @pl.when(t == pl.num_programs(1) - 1)
    def _():
        sfin_ref[0] = s_ref[...]


def _rwkv(pr, prev, s0, w, t_valid):
    batch, seq, _ = pr.shape
    rows = min(RW_ROWS, seq)
    n_pairs = RW_HEADS // 2
    vec = lambda n: _const_spec((1, n))
    return pl.pallas_call(
        functools.partial(_rwkv_kernel, n_chunks=pl.cdiv(rows, RW_CHUNK), t_valid=t_valid),
        grid=(batch, seq // rows),
        in_specs=[pl.BlockSpec((1, rows, RW_COLS), lambda b, t: (b, t, 0)),
                  pl.BlockSpec((1, 1, RW_COLS), lambda b, t: (b, 0, 0)),
                  pl.BlockSpec((1, n_pairs, LANE, LANE), lambda b, t: (b, 0, 0, 0)),
                  vec(RW_COLS), vec(RW_WIDTH), _const_spec((LANE, RW_WIDTH)), _const_spec((LANE, RW_WIDTH)),
                  vec(RW_WIDTH), _const_spec((LANE, RW_WIDTH)), _const_spec((LANE, RW_WIDTH)),
                  vec(RW_WIDTH), vec(RW_WIDTH), vec(RW_WIDTH), vec(RW_WIDTH), vec(RW_WIDTH),
                  _const_spec((LANE, LANE))],
        out_specs=[pl.BlockSpec((1, rows, RW_WIDTH), lambda b, t: (b, t, 0)),
                   pl.BlockSpec((1, n_pairs, LANE, LANE), lambda b, t: (b, 0, 0, 0))],
        out_shape=[jax.ShapeDtypeStruct((batch, seq, RW_WIDTH), BF16),
                   jax.ShapeDtypeStruct((batch, n_pairs, LANE, LANE), F32)],
        scratch_shapes=[pltpu.VMEM((n_pairs, LANE, LANE), F32), pltpu.VMEM((1, RW_COLS), F32)],
        compiler_params=_params(("parallel", "arbitrary")),
        name="rwkv7",
    )(pr, prev, s0, w["mu"], w["w0"], w["w2h"], w["w2l"], w["a0"], w["a2"], w["g2"],
      w["kk"], w["ka"], w["rk"], w["ln_g"], w["ln_b"], w["seg"])


def _even_out_kernel(x_ref, of_ref, yr_ref, wo_ref, g3_ref, gpre_ref, gpost_ref, win_ref, wout_ref,
                     o_ref, act_ref):
    mixed = _dot(of_ref[...], wo_ref[:FOX_WIDTH, :]) + _dot(yr_ref[...], wo_ref[FOX_WIDTH:, :])
    x = x_ref[...] + _rms(mixed, g3_ref[...])
    o_ref[...] = _ffn_apply(x, gpre_ref[...], gpost_ref[...], win_ref, wout_ref, act_ref)


def _even_out(x, of, yr, wo, g3, gpre, gpost, win, wout, widx):
    n = x.shape[0]
    tm = min(ROW_TILE, n)
    vec = _const_spec((1, D_MODEL))
    return pl.pallas_call(
        _even_out_kernel,
        grid=(n // tm,),
        in_specs=[_row_spec(tm, D_MODEL), _row_spec(tm, FOX_WIDTH), _row_spec(tm, RW_WIDTH),
                  _const_spec((D_MODEL, D_MODEL)), vec, vec, vec] + _ffn_weight_specs(widx),
        out_specs=_row_spec(tm, D_MODEL),
        out_shape=jax.ShapeDtypeStruct((n, D_MODEL), F32),
        scratch_shapes=[pltpu.VMEM((tm, D_FF), BF16)],
        compiler_params=_params(("parallel",)),
        name="even_out_ffn",
    )(x, of, yr, wo, g3, gpre, gpost, win, wout)


def _gelu(x):
    return 0.5 * x * (1.0 + lax.erf(x * (2.0 ** -0.5)))


def _odd_kernel(x_ref, g2_ref, win_ref, lng_ref, lnb_ref, wm_ref, bs_ref, wo_ref, g3_ref,
                gpre_ref, gpost_ref, fwin_ref, fwout_ref, o_ref, *rest):
    gv_ref = rest[0] if len(rest) == 3 else None
    act_ref, gate_ref = rest[-2:]
    x = x_ref[...]
    tm = x.shape[0]
    h = _rms(x, g2_ref[...]).astype(BF16)
    v = _gelu(_dot(h, win_ref[:, GM_WIDTH:]))
    vm = jnp.mean(v, axis=-1, keepdims=True)
    vc = v - vm
    var = jnp.mean(vc * vc, axis=-1, keepdims=True)
    v = vc * lax.rsqrt(var + LN_EPS) * lng_ref[...] + lnb_ref[...]
    if gv_ref is not None:
        gv_ref[...] = v
    vb = v.astype(BF16)
    u = _gelu(_dot(h, win_ref[:, :GM_WIDTH]))
    for c in range(tm // GM_CHUNK):
        rows = slice(c * GM_CHUNK, (c + 1) * GM_CHUNK)
        sp = jnp.concatenate(
            [_dot(wm_ref[g], vb[rows, g * LANE:(g + 1) * LANE]) for g in range(GM_GROUPS)], axis=1)
        gate_ref[rows, :] = (u[rows, :] * (sp + bs_ref[...])).astype(BF16)
    mixed = _dot(gate_ref[...], wo_ref[...])
    x = x + _rms(mixed, g3_ref[...])
    o_ref[...] = _ffn_apply(x, gpre_ref[...], gpost_ref[...], fwin_ref, fwout_ref, act_ref)


def _odd(x, g2, win, lng, lnb, wm, bs, wo, g3, gpre, gpost, fwin, fwout, widx, emit_v):
    n = x.shape[0]
    tm = min(ROW_TILE, n)
    vec = _const_spec((1, D_MODEL))
    return pl.pallas_call(
        _odd_kernel,
        grid=(n // tm,),
        in_specs=[_row_spec(tm, D_MODEL), vec, _const_spec((D_MODEL, 2 * GM_WIDTH)), vec, vec,
                  _const_spec((GM_GROUPS, GM_CHUNK, GM_CHUNK)), _const_spec((GM_CHUNK, GM_WIDTH)),
                  _const_spec((GM_WIDTH, D_MODEL)), vec, vec, vec] + _ffn_weight_specs(widx),
        out_specs=[_row_spec(tm, D_MODEL)] + [_row_spec(tm, GM_WIDTH)] * emit_v,
        out_shape=[jax.ShapeDtypeStruct((n, D_MODEL), F32)] + [jax.ShapeDtypeStruct((n, GM_WIDTH), F32)] * emit_v,
        scratch_shapes=[pltpu.VMEM((tm, D_FF), BF16), pltpu.VMEM((tm, GM_WIDTH), BF16)],
        compiler_params=_params(("parallel",)),
        name="odd_mixer_ffn",
    )(x, g2, win, lng, lnb, wm, bs, wo, g3, gpre, gpost, fwin, fwout)


def _pair_block_diag(s):
    b = s.shape[0]
    s = s.reshape(b, RW_HEADS // 2, 2, HEAD_DIM, HEAD_DIM)
    z = jnp.zeros_like(s[:, :, 0])
    top = jnp.concatenate([s[:, :, 0], z], axis=-1)
    bot = jnp.concatenate([z, s[:, :, 1]], axis=-1)
    return jnp.concatenate([top, bot], axis=-2)


def _pair_unblock(s):
    b = s.shape[0]
    h0 = s[:, :, :HEAD_DIM, :HEAD_DIM]
    h1 = s[:, :, HEAD_DIM:, HEAD_DIM:]
    return jnp.stack([h0, h1], axis=2).reshape(b, RW_HEADS, HEAD_DIM, HEAD_DIM)


def kernel(x_prompt, x_sample, cache_fox_k, cache_fox_v, cache_fox_logf, state_rwkv, state_rwkv_shift,
           norm_g, ffn_w_in, ffn_w_out, even_w_in, fox_bf, rw_mu, rw_w0, rw_w2, rw_a0, rw_a2, rw_g2,
           rw_kk, rw_ka, rw_rk, rw_ln_g, rw_ln_b, even_w_out, gm_w_in, gm_ln_g, gm_ln_b, gm_w_s,
           gm_b_s, gm_w_out):
    batch, seq, _ = x_prompt.shape
    dbatch, dseq, _ = x_sample.shape
    past = cache_fox_k.shape[2]
    depth = norm_g.shape[0]
    xp = x_prompt.reshape(batch * seq, D_MODEL)
    xs = x_sample.reshape(dbatch * dseq, D_MODEL)
    ng = lambda layer, i: norm_g[layer, i].reshape(1, D_MODEL)
    win = ffn_w_in.astype(BF16).reshape(depth * 2, D_MODEL, 2 * D_FF)
    wout = ffn_w_out.astype(BF16).reshape(depth * 2, D_FF, D_MODEL)

    fk_p, fv_p, fl_p, rs_p, rsh_p = [], [], [], [], []
    fk_s, fv_s, fl_s, rs_s, rsh_s = [], [], [], [], []
    gv_s = []
    for layer in range(depth):
        j = layer // 2
        if layer % 2 == 0:
            xp = _ffn(xp, ng(layer, 0), ng(layer, 1), win, wout, 2 * layer)
            xs = _ffn(xs, ng(layer, 0), ng(layer, 1), win, wout, 2 * layer)
            ew = even_w_in[j]
            wqkv = ew[:, :3 * FOX_WIDTH].astype(BF16)
            wf = jnp.pad(ew[:, 3 * FOX_WIDTH:FOX_COLS], ((0, 0), (0, LANE - FOX_HEADS))).astype(BF16)
            bf = jnp.pad(fox_bf[j], (0, LANE - FOX_HEADS)).reshape(1, LANE)
            wrw = ew[:, FOX_COLS:].astype(BF16)
            w2 = jnp.pad(rw_w2[j], ((0, LANE - rw_w2.shape[1]), (0, 0)))
            w2h = w2.astype(BF16)
            head_of = jnp.arange(LANE) // HEAD_DIM
            rw = dict(
                mu=rw_mu[j].reshape(1, RW_COLS), w0=rw_w0[j].reshape(1, RW_WIDTH),
                w2h=w2h, w2l=(w2 - w2h.astype(F32)).astype(BF16),
                a0=rw_a0[j].reshape(1, RW_WIDTH),
                a2=jnp.pad(rw_a2[j], ((LANE - rw_a2.shape[1], 0), (0, 0))).astype(BF16),
                g2=rw_g2[j].astype(BF16),
                kk=rw_kk[j].reshape(1, RW_WIDTH), ka=rw_ka[j].reshape(1, RW_WIDTH),
                rk=rw_rk[j].reshape(1, RW_WIDTH),
                ln_g=rw_ln_g[j].reshape(1, RW_WIDTH), ln_b=rw_ln_b[j].reshape(1, RW_WIDTH),
                seg=(head_of[:, None] == head_of[None, :]).astype(BF16))
            wo = even_w_out[j].astype(BF16)

            q, kb, vb, k, v, lf, lfp, pr = _proj(xp, ng(layer, 2), wqkv, wf, bf, wrw)
            of = _fox_prompt(q, kb, vb, lfp, batch, seq)
            pr3 = pr.reshape(batch, seq, RW_COLS)
            yr, sfin = _rwkv(pr3, jnp.zeros((batch, 1, RW_COLS), F32),
                             jnp.zeros((batch, RW_HEADS // 2, LANE, LANE), F32), rw, seq)
            xp = _even_out(xp, of, yr.reshape(batch * seq, RW_WIDTH), wo, ng(layer, 3),
                           ng(layer, 4), ng(layer, 5), win, wout, 2 * layer + 1)
            fk_p.append(k.reshape(batch, seq, FOX_HEADS, HEAD_DIM))
            fv_p.append(v.reshape(batch, seq, FOX_HEADS, HEAD_DIM))
            fl_p.append(lf.reshape(batch, seq, FOX_HEADS))
            rs_p.append(_pair_unblock(sfin))
            rsh_p.append(pr3[:, seq - 1:, :])

            q, kb, vb, k, v, lf, _, pr = _proj(xs, ng(layer, 2), wqkv, wf, bf, wrw)
            lfn_t = jnp.swapaxes(lf.reshape(dbatch, dseq, FOX_HEADS), 1, 2)
            lfc_t = jnp.swapaxes(cache_fox_logf[j], 1, 2)
            of = _fox_sample(q, kb, vb, lfn_t, jnp.transpose(cache_fox_k[j], (0, 2, 3, 1)),
                             jnp.transpose(cache_fox_v[j], (0, 2, 3, 1)), lfc_t, dbatch, dseq, past)
            pr3 = pr.reshape(dbatch, dseq, RW_COLS)
            yr, sfin = _rwkv(pr3, state_rwkv_shift[j], _pair_block_diag(state_rwkv[j]), rw, dseq)
            xs = _even_out(xs, of, yr.reshape(dbatch * dseq, RW_WIDTH), wo, ng(layer, 3),
                           ng(layer, 4), ng(layer, 5), win, wout, 2 * layer + 1)
            fk_s.append(k.reshape(dbatch, dseq, FOX_HEADS, HEAD_DIM))
            fv_s.append(v.reshape(dbatch, dseq, FOX_HEADS, HEAD_DIM))
            fl_s.append(lf.reshape(dbatch, dseq, FOX_HEADS))
            rs_s.append(_pair_unblock(sfin))
            rsh_s.append(pr3[:, dseq - 1:, :])
        else:
            xp = _ffn(xp, ng(layer, 0), ng(layer, 1), win, wout, 2 * layer)
            xs = _ffn(xs, ng(layer, 0), ng(layer, 1), win, wout, 2 * layer)
            cpos = jnp.arange(GM_CHUNK) // CHUNK
            wm = jnp.where(cpos[None, :] <= cpos[:, None], gm_w_s[j], 0.0)
            bs = jnp.repeat(gm_b_s[j].T, GM_WIDTH // GM_GROUPS, axis=1)
            reps = GM_CHUNK // dseq
            wm_s = jnp.einsum('ab,gts->gatbs', jnp.eye(reps, dtype=F32), wm[:, :dseq, :dseq])
            wm_s = wm_s.reshape(GM_GROUPS, GM_CHUNK, GM_CHUNK)
            bs_s = jnp.tile(bs[:dseq], (reps, 1))
            odd = (ng(layer, 2), gm_w_in[j].astype(BF16), gm_ln_g[j].reshape(1, GM_WIDTH),
                   gm_ln_b[j].reshape(1, GM_WIDTH))
            tail = (gm_w_out[j].astype(BF16), ng(layer, 3), ng(layer, 4), ng(layer, 5),
                    win, wout, 2 * layer + 1)
            xp, = _odd(xp, *odd, wm.astype(BF16), bs, *tail, emit_v=False)
            xs, gv = _odd(xs, *odd, wm_s.astype(BF16), bs_s, *tail, emit_v=True)
            gv_s.append(gv.reshape(dbatch, dseq, GM_WIDTH))
    return (xp.reshape(batch, seq, D_MODEL), xs.reshape(dbatch, dseq, D_MODEL),
            jnp.stack(fk_p), jnp.stack(fv_p), jnp.stack(fl_p), jnp.stack(rs_p), jnp.stack(rsh_p),
            jnp.stack(fk_s), jnp.stack(fv_s), jnp.stack(fl_s), jnp.stack(rs_s), jnp.stack(rsh_s),
            jnp.stack(gv_s))
```

```python
import functools
import math

import jax
import jax.numpy as jnp
from jax import lax
from jax.experimental import pallas as pl
from jax.experimental.pallas import tpu as pltpu

F32 = jnp.float32
BF16 = jnp.bfloat16

D_MODEL = 1024
D_FF = 2816
HEAD_DIM = 64
FOX_WIDTH = 512
FOX_HEADS = 8
RW_WIDTH = 512
RW_HEADS = 8
RW_COLS = 1792
RW_OFF_W = 1536
RW_OFF_G = 1664
FOX_COLS = 3 * FOX_WIDTH + FOX_HEADS
GM_WIDTH = 1024
GM_CHUNK = 128
GM_GROUPS = 8
CHUNK = 64
FFN_RES_SCALE = 0.5
RMS_EPS = 1e-6
LN_EPS = 1e-5
GN_EPS = 64e-5

LANE = 128
ROW_TILE = 512
FFN_ROW_TILE = 1024
FF_CHUNK = 256
FOX_BLOCK = 256
FOX_VROWS = HEAD_DIM + 16
LOG2E = 1.4426950408889634
RW_CHUNK = 64
RW_ROWS = 1024
RW_GROUP = 2
NEG_BIG = -1e30
VMEM_LIMIT = 56 * 1024 * 1024


def _params(sem):
    return pltpu.CompilerParams(dimension_semantics=sem, vmem_limit_bytes=VMEM_LIMIT)


def _const_spec(shape):
    n = len(shape)
    return pl.BlockSpec(shape, lambda *_: (0,) * n, pipeline_mode=pl.Buffered(1))


def _dot(a, b):
    return jnp.dot(a, b, preferred_element_type=F32)


def _dot_nt(a, b):
    return lax.dot_general(a, b, (((1,), (1,)), ((), ())), preferred_element_type=F32)


def _dot_tn(a, b):
    return lax.dot_general(a, b, (((0,), (0,)), ((), ())), preferred_element_type=F32)


def _split2(x):
    hi = x.astype(BF16)
    lo = (x - hi.astype(F32)).astype(BF16)
    return hi, lo


def _split3(x):
    hi = x.astype(BF16)
    r1 = x - hi.astype(F32)
    mid = r1.astype(BF16)
    lo = (r1 - mid.astype(F32)).astype(BF16)
    return hi, mid, lo


def _rms(x, g):
    return x * lax.rsqrt(jnp.mean(x * x, axis=-1, keepdims=True) + RMS_EPS) * g


def _sigmoid(x):
    return 1.0 / (1.0 + jnp.exp(-x))


def _log_sigmoid(z):
    return jnp.minimum(z, 0.0) - jnp.log1p(jnp.exp(-jnp.abs(z)))


def _ffn_apply(x, gpre, gpost, win_ref, wout_ref, act_ref):
    h = _rms(x, gpre).astype(BF16)
    for c in range(D_FF // FF_CHUNK):
        lo = c * FF_CHUNK
        gate = _dot(h, win_ref[:, lo:lo + FF_CHUNK])
        up = _dot(h, win_ref[:, D_FF + lo:D_FF + lo + FF_CHUNK])
        act_ref[:, lo:lo + FF_CHUNK] = (gate * _sigmoid(gate) * up).astype(BF16)
    y = _dot(act_ref[...], wout_ref[...])
    return x + FFN_RES_SCALE * _rms(y, gpost)


def _ffn_kernel(x_ref, gpre_ref, gpost_ref, win_ref, wout_ref, o_ref, act_ref):
    o_ref[...] = _ffn_apply(x_ref[...], gpre_ref[...], gpost_ref[...], win_ref, wout_ref, act_ref)


def _row_spec(tm, width):
    return pl.BlockSpec((tm, width), lambda i: (i, 0))


def _ffn_weight_specs(widx):
    pick = lambda *_: (widx, 0, 0)
    return [pl.BlockSpec((None, D_MODEL, 2 * D_FF), pick, pipeline_mode=pl.Buffered(1)),
            pl.BlockSpec((None, D_FF, D_MODEL), pick, pipeline_mode=pl.Buffered(1))]


def _ffn(x, gpre, gpost, win, wout, widx):
    n = x.shape[0]
    tm = min(FFN_ROW_TILE, n)
    return pl.pallas_call(
        _ffn_kernel,
        grid=(n // tm,),
        in_specs=[_row_spec(tm, D_MODEL), _const_spec((1, D_MODEL)), _const_spec((1, D_MODEL))]
        + _ffn_weight_specs(widx),
        out_specs=_row_spec(tm, D_MODEL),
        out_shape=jax.ShapeDtypeStruct((n, D_MODEL), F32),
        scratch_shapes=[pltpu.VMEM((tm, D_FF), BF16)],
        compiler_params=_params(("parallel",)),
        name="ffn",
    )(x, gpre, gpost, win, wout)


def _head_major(x):
    per_head = []
    for g in range(x.shape[1] // LANE):
        xg = x[:, g * LANE:(g + 1) * LANE]
        per_head += [xg, pltpu.roll(xg, HEAD_DIM, 1)]
    return pltpu.einshape("htd->thd", jnp.stack(per_head, axis=0))[:, :, :HEAD_DIM]


def _proj_kernel(x_ref, g_ref, wqkv_ref, wf_ref, bf_ref, wrw_ref,
                 q_ref, kb_ref, vb_ref, k_ref, v_ref, lf_ref, lfp_ref, pr_ref, *, time_minor):
    h = _rms(x_ref[...], g_ref[...]).astype(BF16)
    q_ref[...] = _dot(h, wqkv_ref[:, :FOX_WIDTH]).astype(BF16)
    k = _dot(h, wqkv_ref[:, FOX_WIDTH:2 * FOX_WIDTH])
    v = _dot(h, wqkv_ref[:, 2 * FOX_WIDTH:])
    kb_ref[...] = k.astype(BF16)
    vb_ref[...] = v.astype(BF16)
    lf = _log_sigmoid(_dot(h, wf_ref[...]) + bf_ref[...])
    if time_minor:
        k_ref[0] = k.T
        v_ref[0] = v.T
        lf_ref[0] = lf.T[:FOX_HEADS, :]
    else:
        k_ref[...] = _head_major(k)
        v_ref[...] = _head_major(v)
        lf_ref[...] = lf[:, :FOX_HEADS]
    lfp_ref[...] = lf
    pr_ref[...] = _dot(h, wrw_ref[...])


def _proj(x, g, wqkv, wf, bf, wrw, seq=None):
    n = x.shape[0]
    tm = min(ROW_TILE, n)
    if seq is None:
        cache_specs = [pl.BlockSpec((tm, FOX_HEADS, HEAD_DIM), lambda i: (i, 0, 0))] * 2 + [_row_spec(tm, FOX_HEADS)]
        cache_shapes = [jax.ShapeDtypeStruct((n, FOX_HEADS, HEAD_DIM), F32)] * 2 + [
            jax.ShapeDtypeStruct((n, FOX_HEADS), F32)]
    else:
        per = seq // tm
        pick = lambda i: (i // per, 0, i % per)
        cache_specs = [pl.BlockSpec((1, FOX_WIDTH, tm), pick)] * 2 + [pl.BlockSpec((1, FOX_HEADS, tm), pick)]
        cache_shapes = [jax.ShapeDtypeStruct((n // seq, FOX_WIDTH, seq), F32)] * 2 + [
            jax.ShapeDtypeStruct((n // seq, FOX_HEADS, seq), F32)]
    return pl.pallas_call(
        functools.partial(_proj_kernel, time_minor=seq is not None),
        grid=(n // tm,),
        in_specs=[_row_spec(tm, D_MODEL), _const_spec((1, D_MODEL)),
                  _const_spec((D_MODEL, 3 * FOX_WIDTH)), _const_spec((D_MODEL, LANE)),
                  _const_spec((1, LANE)), _const_spec((D_MODEL, RW_COLS))],
        out_specs=[_row_spec(tm, FOX_WIDTH), _row_spec(tm, FOX_WIDTH), _row_spec(tm, FOX_WIDTH)]
        + cache_specs + [_row_spec(tm, LANE), _row_spec(tm, RW_COLS)],
        out_shape=[jax.ShapeDtypeStruct((n, FOX_WIDTH), BF16)] * 3 + cache_shapes + [
            jax.ShapeDtypeStruct((n, LANE), F32), jax.ShapeDtypeStruct((n, RW_COLS), F32)],
        compiler_params=_params(("parallel",)),
        name="even_proj",
    )(x, g, wqkv, wf, bf, wrw)


def _lane_cumsum(rows, carry):
    r = lax.broadcasted_iota(jnp.int32, (LANE, LANE), 0)
    c = lax.broadcasted_iota(jnp.int32, (LANE, LANE), 1)
    upper = (r <= c).astype(BF16)
    hi, mid, lo = _split3(rows)
    return _dot(hi, upper) + _dot(mid, upper) + _dot(lo, upper) + carry


def _fox_prompt_kernel(q_ref, k_ref, v_ref, lfp_ref, o_ref, kx_ref, vt_ref, qx_ref, m_ref, alpha_ref, p_ref,
                       acc_ref, *, seq):
    i = pl.program_id(1)
    bq = FOX_BLOCK
    n_pairs = FOX_HEADS // 2
    heads = range(FOX_HEADS)
    val_rows = [slice(h * FOX_VROWS, h * FOX_VROWS + HEAD_DIM) for h in heads]
    ext_rows = [slice(h * FOX_VROWS, (h + 1) * FOX_VROWS) for h in heads]

    @pl.when(i == 0)
    def _():
        r = lax.broadcasted_iota(jnp.int32, (LANE, LANE), 0)
        c = lax.broadcasted_iota(jnp.int32, (LANE, LANE), 1)
        tri = (c <= r).astype(BF16)
        ones_row = (r[:FOX_VROWS - HEAD_DIM] == 0).astype(BF16)
        carry = jnp.zeros((1, LANE), F32)
        for blk in range(seq // LANE):
            rows = slice(blk * LANE, (blk + 1) * LANE)
            hi, mid, lo = _split3(jnp.where(c < FOX_HEADS, lfp_ref[rows, :], 0.0))
            cum = _dot(tri, hi) + _dot(tri, mid) + _dot(tri, lo) + carry
            carry = cum[LANE - 1:LANE, :]
            chi, cmid, clo = _split3(cum * LOG2E)
            feat = (chi.astype(F32) + pltpu.roll(cmid.astype(F32), FOX_HEADS, 1)
                    + pltpu.roll(clo.astype(F32), 2 * FOX_HEADS, 1)).astype(BF16)
            for p in range(n_pairs):
                lanes = slice(p * LANE, (p + 1) * LANE)
                kx_ref[p, rows, :LANE] = k_ref[rows, lanes]
                kx_ref[p, rows, LANE:] = feat
                vt = v_ref[rows, lanes].astype(F32).T.astype(BF16)
                vt_ref[val_rows[2 * p], rows] = vt[:HEAD_DIM]
                vt_ref[val_rows[2 * p + 1], rows] = vt[HEAD_DIM:]
            for h in heads:
                vt_ref[h * FOX_VROWS + HEAD_DIM:(h + 1) * FOX_VROWS, rows] = ones_row

    q0 = pl.multiple_of(i * bq, bq)
    lane = lax.broadcasted_iota(jnp.int32, (1, LANE), 1)
    krow = lax.broadcasted_iota(jnp.int32, (bq, bq), 0)
    qcol = lax.broadcasted_iota(jnp.int32, (bq, bq), 1)
    causal = krow <= qcol
    for p in range(n_pairs):
        qp = q_ref[:, p * LANE:(p + 1) * LANE].astype(F32) * (HEAD_DIM ** -0.5 * LOG2E)
        for hh in range(2):
            h = 2 * p + hh
            in_head = (lane >= HEAD_DIM) if hh else (lane < HEAD_DIM)
            pick = (lane == h) | (lane == h + FOX_HEADS) | (lane == h + 2 * FOX_HEADS)
            sel = jnp.where(pick, -1.0, 0.0).astype(BF16)
            qx_ref[h, :, :LANE] = jnp.where(in_head, qp, 0.0).astype(BF16)
            qx_ref[h, :, LANE:] = jnp.broadcast_to(sel, (bq, LANE))

    def scores(j0):
        return [_dot_nt(kx_ref[h // 2, pl.ds(j0, bq), :], qx_ref[h]) for h in heads]

    def softmax(sts, diag):
        for h in heads:
            st = jnp.where(causal, sts[h], NEG_BIG) if diag else sts[h]
            m_old = m_ref[h:h + 1, :]
            m_new = jnp.maximum(m_old, jnp.max(st, axis=0, keepdims=True))
            alpha_ref[h:h + 1, :] = jnp.exp2(m_old - m_new)
            m_ref[h:h + 1, :] = m_new
            p_ref[h] = jnp.exp2(st - m_new).astype(BF16)

    def accumulate(j0):
        pvs = [_dot(vt_ref[ext_rows[h], pl.ds(j0, bq)], p_ref[h]) for h in heads]
        for h in heads:
            acc_ref[ext_rows[h], :] = alpha_ref[h:h + 1, :] * acc_ref[ext_rows[h], :] + pvs[h]

    m_ref[...] = jnp.full(m_ref.shape, NEG_BIG, F32)
    acc_ref[...] = jnp.zeros(acc_ref.shape, F32)
    softmax(scores(q0), True)

    def body(j, _):
        j0 = pl.multiple_of(j * bq, bq)
        sts = scores(j0)
        accumulate(pl.multiple_of(jnp.where(j == 0, q0, j0 - bq), bq))
        softmax(sts, False)
        return 0

    lax.fori_loop(0, i, body, 0)
    accumulate(pl.multiple_of(jnp.where(i == 0, q0, q0 - bq), bq))
    out = [acc_ref[val_rows[h], :] * (1.0 / acc_ref[h * FOX_VROWS + HEAD_DIM:h * FOX_VROWS + HEAD_DIM + 1, :])
           for h in heads]
    o_ref[...] = jnp.concatenate(out, axis=0).T.astype(BF16)


def _fox_prompt(q, k, v, lfp, batch, seq):
    nq = seq // FOX_BLOCK
    return pl.pallas_call(
        functools.partial(_fox_prompt_kernel, seq=seq),
        grid=(batch, nq),
        in_specs=[pl.BlockSpec((FOX_BLOCK, FOX_WIDTH), lambda b, i: (b * nq + i, 0)),
                  pl.BlockSpec((seq, FOX_WIDTH), lambda b, i: (b, 0)),
                  pl.BlockSpec((seq, FOX_WIDTH), lambda b, i: (b, 0)),
                  pl.BlockSpec((seq, LANE), lambda b, i: (b, 0))],
        out_specs=pl.BlockSpec((FOX_BLOCK, FOX_WIDTH), lambda b, i: (b * nq + i, 0)),
        out_shape=jax.ShapeDtypeStruct((batch * seq, FOX_WIDTH), BF16),
        scratch_shapes=[pltpu.VMEM((FOX_HEADS // 2, seq, 2 * LANE), BF16),
                        pltpu.VMEM((FOX_HEADS * FOX_VROWS, seq), BF16),
                        pltpu.VMEM((FOX_HEADS, FOX_BLOCK, 2 * LANE), BF16),
                        pltpu.VMEM((FOX_HEADS, FOX_BLOCK), F32),
                        pltpu.VMEM((FOX_HEADS, FOX_BLOCK), F32),
                        pltpu.VMEM((FOX_HEADS, FOX_BLOCK, FOX_BLOCK), BF16),
                        pltpu.VMEM((FOX_HEADS * FOX_VROWS, FOX_BLOCK), F32)],
        compiler_params=_params(("parallel", "arbitrary")),
        name="fox_prompt",
    )(q, k, v, lfp)


def _fox_sample_kernel(q_ref, kn_ref, vn_ref, lfn_ref, kc_ref, vc_ref, lfc_ref, o_ref,
                       kb_ref, vb_ref, *, past, tnew):
    heads = range(FOX_HEADS)
    hrows = [slice(h * HEAD_DIM, (h + 1) * HEAD_DIM) for h in heads]
    for h in heads:
        kb_ref[hrows[h], :] = kc_ref[0, h].astype(BF16)
        vb_ref[hrows[h], :] = vc_ref[0, h].astype(BF16)
    q = q_ref[...].astype(F32) * (HEAD_DIM ** -0.5)
    qh = [q[:, hrows[h]].astype(BF16) for h in heads]

    carry = jnp.zeros((FOX_HEADS, 1), F32)
    cts = []
    for b in range(past // LANE):
        c = _lane_cumsum(lfc_ref[0, :, b * LANE:(b + 1) * LANE], carry)
        cts.append(c)
        carry = c[:, LANE - 1:LANE]
    total = carry
    r = lax.broadcasted_iota(jnp.int32, (tnew, tnew), 0)
    c = lax.broadcasted_iota(jnp.int32, (tnew, tnew), 1)
    hi, mid, lo = _split3(lfn_ref[0])
    upper = (r <= c).astype(BF16)
    cnew = _dot(hi, upper) + _dot(mid, upper) + _dot(lo, upper)

    ccache = jnp.concatenate(cts, axis=1) - total
    kn, vn = kn_ref[...], vn_ref[...]
    sc = [_dot(qh[h], kb_ref[hrows[h], :]) - ccache[h:h + 1, :] for h in heads]
    sn = [jnp.where(c <= r, _dot_nt(qh[h], kn[:, hrows[h]]) - cnew[h:h + 1, :], NEG_BIG) for h in heads]
    pcs, pns, ls = [], [], []
    for h in heads:
        m = jnp.maximum(jnp.max(sc[h], axis=1, keepdims=True), jnp.max(sn[h], axis=1, keepdims=True))
        pc = jnp.exp(sc[h] - m)
        pn = jnp.exp(sn[h] - m)
        ls.append(jnp.sum(pc, axis=1, keepdims=True) + jnp.sum(pn, axis=1, keepdims=True))
        pcs.append(pc.astype(BF16))
        pns.append(pn.astype(BF16))
    outs = [(_dot_nt(pcs[h], vb_ref[hrows[h], :]) + _dot(pns[h], vn[:, hrows[h]])) / ls[h] for h in heads]
    o_ref[...] = jnp.concatenate(outs, axis=1).astype(BF16)


def _fox_sample(q, kn, vn, lfn_t, kc, vc, lfc_t, batch, tnew, past):
    return pl.pallas_call(
        functools.partial(_fox_sample_kernel, past=past, tnew=tnew),
        grid=(batch,),
        in_specs=[pl.BlockSpec((tnew, FOX_WIDTH), lambda b: (b, 0)),
                  pl.BlockSpec((tnew, FOX_WIDTH), lambda b: (b, 0)),
                  pl.BlockSpec((tnew, FOX_WIDTH), lambda b: (b, 0)),
                  pl.BlockSpec((1, FOX_HEADS, tnew), lambda b: (b, 0, 0)),
                  pl.BlockSpec((1, FOX_HEADS, HEAD_DIM, past), lambda b: (b, 0, 0, 0)),
                  pl.BlockSpec((1, FOX_HEADS, HEAD_DIM, past), lambda b: (b, 0, 0, 0)),
                  pl.BlockSpec((1, FOX_HEADS, past), lambda b: (b, 0, 0))],
        out_specs=pl.BlockSpec((tnew, FOX_WIDTH), lambda b: (b, 0)),
        out_shape=jax.ShapeDtypeStruct((batch * tnew, FOX_WIDTH), BF16),
        scratch_shapes=[pltpu.VMEM((FOX_WIDTH, past), BF16), pltpu.VMEM((FOX_WIDTH, past), BF16)],
        compiler_params=_params(("parallel",)),
        name="fox_sample",
    )(q, kn, vn, lfn_t, kc, vc, lfc_t)


def _seg_sum(z, e_ref):
    zb = z.astype(BF16)
    return jnp.concatenate(
        [_dot(zb[:, p * LANE:(p + 1) * LANE], e_ref[...]) for p in range(z.shape[1] // LANE)], axis=1)


def _rwkv_kernel(pr_ref, prev_ref, s0_ref, mu_ref, w0_ref, w2h_ref, w2l_ref, a0_ref, a2_ref, g2_ref,
                 kk_ref, ka_ref, rk_ref, lng_ref, lnb_ref, e_ref,
                 y_ref, sfin_ref, s_ref, pv_ref, *, n_chunks, t_valid):
    cc = RW_CHUNK
    t = pl.program_id(1)

    @pl.when(t == 0)
    def _():
        s_ref[...] = s0_ref[0]
        pv_ref[...] = prev_ref[0]

    rows = n_chunks * cc
    gsize = min(RW_GROUP, n_chunks)
    groups = [list(range(c0, c0 + gsize)) for c0 in range(0, n_chunks, gsize)]
    gr = gsize * cc
    ri = lax.broadcasted_iota(jnp.int32, (gr, gr), 0)
    ci = lax.broadcasted_iota(jnp.int32, (gr, gr), 1)
    tri = ((ci <= ri) & (ci // cc == ri // cc)).astype(BF16)
    r2 = lax.broadcasted_iota(jnp.int32, (2 * cc, 2 * cc), 0) % cc
    c2 = lax.broadcasted_iota(jnp.int32, (2 * cc, 2 * cc), 1) % cc
    strict = c2 < r2
    incl = c2 <= r2
    row1 = lax.broadcasted_iota(jnp.int32, (rows, 1), 0)

    pr = pr_ref[0]
    n_real = pr.shape[0]
    if n_real < rows:
        pr = jnp.concatenate([pr, jnp.zeros((rows - n_real, RW_COLS), F32)], axis=0)
    prev_seq = jnp.where(row1 == 0, pv_ref[...], pltpu.roll(pr, 1, 0))
    pv_ref[...] = pr[n_real - 1:n_real, :]
    xm = pr + (prev_seq - pr) * mu_ref[...]
    tok = {}
    cmids = {}

    def tw_project(chunks):
        x = xm[chunks[0] * cc:(chunks[-1] + 1) * cc]
        d = tok[chunks[0]] = dict(r=x[:, :RW_WIDTH], kx=x[:, RW_WIDTH:2 * RW_WIDTH], vx=x[:, 2 * RW_WIDTH:RW_OFF_W])
        xwa = x[:, RW_OFF_W:RW_OFF_G]
        th, tl = _split2(jnp.tanh(xwa))
        d["zw"] = w0_ref[...] + (_dot(th, w2h_ref[...]) + _dot(th, w2l_ref[...]) + _dot(tl, w2h_ref[...]))
        d["za"] = a0_ref[...] + _dot(xwa.astype(BF16), a2_ref[...])
        d["g"] = _dot(_sigmoid(x[:, RW_OFF_G:]).astype(BF16), g2_ref[...])

    def tw_decay(chunks):
        d = tok[chunks[0]]
        lw = -math.exp(-0.5) * _sigmoid(d.pop("zw"))
        asig = _sigmoid(d.pop("za"))
        kkv = d["kx"] * kk_ref[...]
        kkn = kkv / jnp.maximum(jnp.sqrt(_seg_sum(kkv * kkv, e_ref)), 1e-12)
        d["kmod"] = d.pop("kx") * (1.0 + (asig - 1.0) * ka_ref[...])
        d["a_"], d["b_"], d["k_"] = -kkn, kkn * asig, d["kmod"]
        if t_valid < rows:
            valid = row1[chunks[0] * cc:(chunks[-1] + 1) * cc] < t_valid
            lw = jnp.where(valid, lw, 0.0)
            d["b_"] = jnp.where(valid, d["b_"], 0.0)
            d["k_"] = jnp.where(valid, d["k_"], 0.0)
        hi, mid, lo = _split3(lw)
        d["lw"] = lw
        d["cum"] = _dot(tri, hi) + _dot(tri, mid) + _dot(tri, lo)

    def tw_scale(chunks):
        d = tok[chunks[0]]
        cum = d["cum"]
        for i, c in enumerate(chunks):
            cmids[c] = cum[i * cc + cc // 2 - 1:i * cc + cc // 2, :]
        cmid = jnp.concatenate([jnp.broadcast_to(cmids[c], (cc, RW_WIDTH)) for c in chunks], axis=0)
        d["at"] = (d.pop("a_") * jnp.exp(cum - d.pop("lw") - cmid)).astype(BF16)
        d["rt"] = (d["r"] * jnp.exp(cum - cmid)).astype(BF16)
        e_neg = jnp.exp(cmid - cum)
        d["bt"] = (d.pop("b_") * e_neg).astype(BF16)
        d["kt"] = (d.pop("k_") * e_neg).astype(BF16)
        d["vb"] = d["vx"].astype(BF16)

    def token_stages(chunks):
        return [functools.partial(f, chunks) for f in (tw_project, tw_decay, tw_scale)]

    lane = lax.broadcasted_iota(jnp.int32, (cc, LANE), 1)
    zero = jnp.zeros((cc, LANE), BF16)

    def stack(x):
        return jnp.concatenate([jnp.where(lane < HEAD_DIM, x, zero), jnp.where(lane >= HEAD_DIM, x, zero)], axis=0)

    n2 = 2 * cc
    n_pairs = RW_HEADS // 2
    pairs = range(n_pairs)
    eye = (lax.broadcasted_iota(jnp.int32, (n2, n2), 0) == lax.broadcasted_iota(jnp.int32, (n2, n2), 1)).astype(F32)
    dat = {(c, p): {} for c in range(n_chunks) for p in pairs}

    def group_of(c):
        return c - c % gsize

    def st_scores(cps):
        for c, p in cps:
            tk = tok[group_of(c)]
            off = (c - group_of(c)) * cc
            rs, lanes = slice(off, off + cc), slice(p * LANE, (p + 1) * LANE)
            d = dat[c, p]
            d["a"], d["r"], b, k, d["v"] = [
                stack(tk[name][rs, lanes]) for name in ("at", "rt", "bt", "kt", "vb")]
            d["bk"] = jnp.concatenate([b, k], axis=0)
        for cp in cps:
            d = dat[cp]
            d["m"] = _dot_nt(jnp.concatenate([d["a"], d["r"]], axis=0), d["bk"])

    def st_local(cps):
        for cp in cps:
            d = dat[cp]
            m = d.pop("m")
            d["l"] = jnp.where(strict, m[:n2, :n2], 0.0)
            d["mrb"] = jnp.where(incl, m[n2:, :n2], 0.0).astype(BF16)
            mk = jnp.concatenate([jnp.where(strict, m[:n2, n2:], 0.0), jnp.where(incl, m[n2:, n2:], 0.0)], axis=0)
            mv = _dot(mk.astype(BF16), d["v"])
            d["mv"], d["y0"] = mv[:n2].astype(BF16), mv[n2:]
        for cp in cps:
            d = dat[cp]
            lb = d["l"].astype(BF16)
            d["t"] = eye + d["l"]
            d["l"] = _dot(lb, lb)

    def st_double(cps, last):
        for cp in cps:
            d = dat[cp]
            lb, tb = d["l"].astype(BF16), d["t"].astype(BF16)
            if last:
                d["t"] = d["t"] + _dot(lb, tb)
            else:
                z = _dot(lb, jnp.concatenate([lb, tb], axis=1))
                d["l"], d["t"] = z[:, :n2], d["t"] + z[:, n2:]

    def st_solve(cps):
        for cp in cps:
            d = dat[cp]
            wu = _dot(d["t"].astype(BF16), jnp.concatenate([d["a"], d["mv"]], axis=1))
            d["wr"] = jnp.concatenate([wu[:, :LANE].astype(BF16), d["r"]], axis=0)
            d["u0"] = wu[:, LANE:]
            del d["l"], d["t"], d["a"], d["mv"]

    def local_stages(cps):
        n_double = int(math.log2(cc)) - 1
        return ([lambda: st_scores(cps), lambda: st_local(cps)]
                + [functools.partial(st_double, cps, s_i + 1 == n_double) for s_i in range(n_double)]
                + [lambda: st_solve(cps)])

    state = [s_ref[p] for p in pairs]
    ychunks = {}

    def ch_enter(c):
        g_mid = jnp.exp(cmids[c])
        for p in pairs:
            d = dat[c, p]
            d["sp"] = state[p] * g_mid[:, p * LANE:(p + 1) * LANE]
            d["ws"] = _dot_nt(d["wr"], d["sp"].astype(BF16))

    def ch_leave(c):
        off = (c - group_of(c)) * cc
        g_end = jnp.exp(tok[group_of(c)]["cum"][off + cc - 1:off + cc, :] - cmids[c])
        y2 = []
        for p in pairs:
            d = dat[c, p]
            d["ub"] = (d["ws"][:n2] + d["u0"]).astype(BF16)
        for p in pairs:
            d = dat[c, p]
            y2.append(d["ws"][n2:] + _dot(d["mrb"], d["ub"]) + d["y0"])
            upd = _dot_tn(jnp.concatenate([d["ub"], d["v"]], axis=0), d["bk"])
            state[p] = (d["sp"] + upd) * g_end[:, p * LANE:(p + 1) * LANE]
            dat[c, p] = None
        ychunks[c] = jnp.concatenate([t[:cc] + t[cc:] for t in y2], axis=1)

    def finish(chunks):
        tk = tok[chunks[0]]
        y = jnp.concatenate([ychunks[c] for c in chunks], axis=0) if len(chunks) > 1 else ychunks[chunks[0]]
        ym = _seg_sum(y, e_ref) * (1.0 / HEAD_DIM)
        yc = y - ym
        yv = _seg_sum(yc * yc, e_ref) * (1.0 / HEAD_DIM)
        yn = yc * lax.rsqrt(yv + GN_EPS) * lng_ref[...] + lnb_ref[...]
        bonus = _seg_sum(tk["r"] * tk["kmod"] * rk_ref[...], e_ref) * tk["vx"]
        lo = chunks[0] * cc
        hi = min((chunks[-1] + 1) * cc, n_real)
        y_ref[0, lo:hi, :] = ((yn + bonus) * tk["g"]).astype(BF16)[:hi - lo]

    def emit(main, side):
        for stage in main:
            stage()
            if side:
                side.pop(0)()
        while side:
            side.pop(0)()

    def serial_stages(chunks):
        fs = [f for c in chunks for f in (functools.partial(ch_enter, c), functools.partial(ch_leave, c))]
        return fs + [functools.partial(finish, chunks)]

    for stage in token_stages(groups[0]):
        stage()
    for gi, chunks in enumerate(groups):
        nxt = token_stages(groups[gi + 1]) if gi + 1 < len(groups) else []
        prv = serial_stages(groups[gi - 1]) if gi else []
        side = [f for pair in zip(nxt, prv) for f in pair] + nxt[len(prv):] + prv[len(nxt):]
        emit(local_stages([(c, p) for c in chunks for p in pairs]), side)
    for stage in serial_stages(groups[-1]):
        stage()
    for p in pairs:
        s_ref[p] = state[p]

    @pl.when(t == pl.num_programs(1) - 1)
    def _():
        sfin_ref[0] = s_ref[...]


def _rwkv(pr, prev, s0, w, t_valid):
    batch, seq, _ = pr.shape
    rows = min(RW_ROWS, seq)
    n_pairs = RW_HEADS // 2
    vec = lambda n: _const_spec((1, n))
    return pl.pallas_call(
        functools.partial(_rwkv_kernel, n_chunks=pl.cdiv(rows, RW_CHUNK), t_valid=t_valid),
        grid=(batch, seq // rows),
        in_specs=[pl.BlockSpec((1, rows, RW_COLS), lambda b, t: (b, t, 0)),
                  pl.BlockSpec((1, 1, RW_COLS), lambda b, t: (b, 0, 0)),
                  pl.BlockSpec((1, n_pairs, LANE, LANE), lambda b, t: (b, 0, 0, 0)),
                  vec(RW_COLS), vec(RW_WIDTH), _const_spec((LANE, RW_WIDTH)), _const_spec((LANE, RW_WIDTH)),
                  vec(RW_WIDTH), _const_spec((LANE, RW_WIDTH)), _const_spec((LANE, RW_WIDTH)),
                  vec(RW_WIDTH), vec(RW_WIDTH), vec(RW_WIDTH), vec(RW_WIDTH), vec(RW_WIDTH),
                  _const_spec((LANE, LANE))],
        out_specs=[pl.BlockSpec((1, rows, RW_WIDTH), lambda b, t: (b, t, 0)),
                   pl.BlockSpec((1, n_pairs, LANE, LANE), lambda b, t: (b, 0, 0, 0))],
        out_shape=[jax.ShapeDtypeStruct((batch, seq, RW_WIDTH), BF16),
                   jax.ShapeDtypeStruct((batch, n_pairs, LANE, LANE), F32)],
        scratch_shapes=[pltpu.VMEM((n_pairs, LANE, LANE), F32), pltpu.VMEM((1, RW_COLS), F32)],
        compiler_params=_params(("parallel", "arbitrary")),
        name="rwkv7",
    )(pr, prev, s0, w["mu"], w["w0"], w["w2h"], w["w2l"], w["a0"], w["a2"], w["g2"],
      w["kk"], w["ka"], w["rk"], w["ln_g"], w["ln_b"], w["seg"])


def _even_out_kernel(x_ref, of_ref, yr_ref, wo_ref, g3_ref, gpre_ref, gpost_ref, win_ref, wout_ref,
                     o_ref, act_ref):
    mixed = _dot(of_ref[...], wo_ref[:FOX_WIDTH, :]) + _dot(yr_ref[...], wo_ref[FOX_WIDTH:, :])
    x = x_ref[...] + _rms(mixed, g3_ref[...])
    o_ref[...] = _ffn_apply(x, gpre_ref[...], gpost_ref[...], win_ref, wout_ref, act_ref)


def _even_out(x, of, yr, wo, g3, gpre, gpost, win, wout, widx):
    n = x.shape[0]
    tm = min(FFN_ROW_TILE, n)
    vec = _const_spec((1, D_MODEL))
    return pl.pallas_call(
        _even_out_kernel,
        grid=(n // tm,),
        in_specs=[_row_spec(tm, D_MODEL), _row_spec(tm, FOX_WIDTH), _row_spec(tm, RW_WIDTH),
                  _const_spec((D_MODEL, D_MODEL)), vec, vec, vec] + _ffn_weight_specs(widx),
        out_specs=_row_spec(tm, D_MODEL),
        out_shape=jax.ShapeDtypeStruct((n, D_MODEL), F32),
        scratch_shapes=[pltpu.VMEM((tm, D_FF), BF16)],
        compiler_params=_params(("parallel",)),
        name="even_out_ffn",
    )(x, of, yr, wo, g3, gpre, gpost, win, wout)


def _gelu(x):
    return 0.5 * x * (1.0 + lax.erf(x * (2.0 ** -0.5)))


def _odd_kernel(x_ref, g2_ref, win_ref, lng_ref, lnb_ref, wm_ref, bs_ref, wo_ref, g3_ref,
                gpre_ref, gpost_ref, fwin_ref, fwout_ref, o_ref, *rest):
    gv_ref = rest[0] if len(rest) == 3 else None
    act_ref, gate_ref = rest[-2:]
    x = x_ref[...]
    tm = x.shape[0]
    h = _rms(x, g2_ref[...]).astype(BF16)
    v = _gelu(_dot(h, win_ref[:, GM_WIDTH:]))
    vm = jnp.mean(v, axis=-1, keepdims=True)
    vc = v - vm
    var = jnp.mean(vc * vc, axis=-1, keepdims=True)
    v = vc * lax.rsqrt(var + LN_EPS) * lng_ref[...] + lnb_ref[...]
    if gv_ref is not None:
        gv_ref[...] = v
    vb = v.astype(BF16)
    u = _gelu(_dot(h, win_ref[:, :GM_WIDTH]))
    for c in range(tm // GM_CHUNK):
        rows = slice(c * GM_CHUNK, (c + 1) * GM_CHUNK)
        sp = jnp.concatenate(
            [_dot(wm_ref[g], vb[rows, g * LANE:(g + 1) * LANE]) for g in range(GM_GROUPS)], axis=1)
        gate_ref[rows, :] = (u[rows, :] * (sp + bs_ref[...])).astype(BF16)
    mixed = _dot(gate_ref[...], wo_ref[...])
    x = x + _rms(mixed, g3_ref[...])
    o_ref[...] = _ffn_apply(x, gpre_ref[...], gpost_ref[...], fwin_ref, fwout_ref, act_ref)


def _odd(x, g2, win, lng, lnb, wm, bs, wo, g3, gpre, gpost, fwin, fwout, widx, emit_v):
    n = x.shape[0]
    tm = min(ROW_TILE, n)
    vec = _const_spec((1, D_MODEL))
    return pl.pallas_call(
        _odd_kernel,
        grid=(n // tm,),
        in_specs=[_row_spec(tm, D_MODEL), vec, _const_spec((D_MODEL, 2 * GM_WIDTH)), vec, vec,
                  _const_spec((GM_GROUPS, GM_CHUNK, GM_CHUNK)), _const_spec((GM_CHUNK, GM_WIDTH)),
                  _const_spec((GM_WIDTH, D_MODEL)), vec, vec, vec] + _ffn_weight_specs(widx),
        out_specs=[_row_spec(tm, D_MODEL)] + [_row_spec(tm, GM_WIDTH)] * emit_v,
        out_shape=[jax.ShapeDtypeStruct((n, D_MODEL), F32)] + [jax.ShapeDtypeStruct((n, GM_WIDTH), F32)] * emit_v,
        scratch_shapes=[pltpu.VMEM((tm, D_FF), BF16), pltpu.VMEM((tm, GM_WIDTH), BF16)],
        compiler_params=_params(("parallel",)),
        name="odd_mixer_ffn",
    )(x, g2, win, lng, lnb, wm, bs, wo, g3, gpre, gpost, fwin, fwout)


def _pair_block_diag(s):
    b = s.shape[0]
    s = s.reshape(b, RW_HEADS // 2, 2, HEAD_DIM, HEAD_DIM)
    z = jnp.zeros_like(s[:, :, 0])
    top = jnp.concatenate([s[:, :, 0], z], axis=-1)
    bot = jnp.concatenate([z, s[:, :, 1]], axis=-1)
    return jnp.concatenate([top, bot], axis=-2)


def _pair_unblock(s):
    b = s.shape[0]
    h0 = s[:, :, :HEAD_DIM, :HEAD_DIM]
    h1 = s[:, :, HEAD_DIM:, HEAD_DIM:]
    return jnp.stack([h0, h1], axis=2).reshape(b, RW_HEADS, HEAD_DIM, HEAD_DIM)


def kernel(x_prompt, x_sample, cache_fox_k, cache_fox_v, cache_fox_logf, state_rwkv, state_rwkv_shift,
           norm_g, ffn_w_in, ffn_w_out, even_w_in, fox_bf, rw_mu, rw_w0, rw_w2, rw_a0, rw_a2, rw_g2,
           rw_kk, rw_ka, rw_rk, rw_ln_g, rw_ln_b, even_w_out, gm_w_in, gm_ln_g, gm_ln_b, gm_w_s,
           gm_b_s, gm_w_out):
    batch, seq, _ = x_prompt.shape
    dbatch, dseq, _ = x_sample.shape
    past = cache_fox_k.shape[2]
    depth = norm_g.shape[0]
    xp = x_prompt.reshape(batch * seq, D_MODEL)
    xs = x_sample.reshape(dbatch * dseq, D_MODEL)
    ng = lambda layer, i: norm_g[layer, i].reshape(1, D_MODEL)
    win = ffn_w_in.astype(BF16).reshape(depth * 2, D_MODEL, 2 * D_FF)
    wout = ffn_w_out.astype(BF16).reshape(depth * 2, D_FF, D_MODEL)

    fk_p, fv_p, fl_p, rs_p, rsh_p = [], [], [], [], []
    fk_s, fv_s, fl_s, rs_s, rsh_s = [], [], [], [], []
    gv_s = []
    for layer in range(depth):
        j = layer // 2
        if layer % 2 == 0:
            xp = _ffn(xp, ng(layer, 0), ng(layer, 1), win, wout, 2 * layer)
            xs = _ffn(xs, ng(layer, 0), ng(layer, 1), win, wout, 2 * layer)
            ew = even_w_in[j]
            wqkv = ew[:, :3 * FOX_WIDTH].astype(BF16)
            wf = jnp.pad(ew[:, 3 * FOX_WIDTH:FOX_COLS], ((0, 0), (0, LANE - FOX_HEADS))).astype(BF16)
            bf = jnp.pad(fox_bf[j], (0, LANE - FOX_HEADS)).reshape(1, LANE)
            wrw = ew[:, FOX_COLS:].astype(BF16)
            w2 = jnp.pad(rw_w2[j], ((0, LANE - rw_w2.shape[1]), (0, 0)))
            w2h = w2.astype(BF16)
            head_of = jnp.arange(LANE) // HEAD_DIM
            rw = dict(
                mu=rw_mu[j].reshape(1, RW_COLS), w0=rw_w0[j].reshape(1, RW_WIDTH),
                w2h=w2h, w2l=(w2 - w2h.astype(F32)).astype(BF16),
                a0=rw_a0[j].reshape(1, RW_WIDTH),
                a2=jnp.pad(rw_a2[j], ((LANE - rw_a2.shape[1], 0), (0, 0))).astype(BF16),
                g2=rw_g2[j].astype(BF16),
                kk=rw_kk[j].reshape(1, RW_WIDTH), ka=rw_ka[j].reshape(1, RW_WIDTH),
                rk=rw_rk[j].reshape(1, RW_WIDTH),
                ln_g=rw_ln_g[j].reshape(1, RW_WIDTH), ln_b=rw_ln_b[j].reshape(1, RW_WIDTH),
                seg=(head_of[:, None] == head_of[None, :]).astype(BF16))
            wo = even_w_out[j].astype(BF16)

            q, kb, vb, k, v, lf, lfp, pr = _proj(xp, ng(layer, 2), wqkv, wf, bf, wrw, seq=seq)
            of = _fox_prompt(q, kb, vb, lfp, batch, seq)
            pr3 = pr.reshape(batch, seq, RW_COLS)
            yr, sfin = _rwkv(pr3, jnp.zeros((batch, 1, RW_COLS), F32),
                             jnp.zeros((batch, RW_HEADS // 2, LANE, LANE), F32), rw, seq)
            xp = _even_out(xp, of, yr.reshape(batch * seq, RW_WIDTH), wo, ng(layer, 3),
                           ng(layer, 4), ng(layer, 5), win, wout, 2 * layer + 1)
            fk_p.append(jnp.transpose(k.reshape(batch, FOX_HEADS, HEAD_DIM, seq), (0, 3, 1, 2)))
            fv_p.append(jnp.transpose(v.reshape(batch, FOX_HEADS, HEAD_DIM, seq), (0, 3, 1, 2)))
            fl_p.append(jnp.transpose(lf, (0, 2, 1)))
            rs_p.append(_pair_unblock(sfin))
            rsh_p.append(pr3[:, seq - 1:, :])

            q, kb, vb, k, v, lf, _, pr = _proj(xs, ng(layer, 2), wqkv, wf, bf, wrw)
            lfn_t = jnp.swapaxes(lf.reshape(dbatch, dseq, FOX_HEADS), 1, 2)
            lfc_t = jnp.swapaxes(cache_fox_logf[j], 1, 2)
            of = _fox_sample(q, kb, vb, lfn_t, jnp.transpose(cache_fox_k[j], (0, 2, 3, 1)),
                             jnp.transpose(cache_fox_v[j], (0, 2, 3, 1)), lfc_t, dbatch, dseq, past)
            pr3 = pr.reshape(dbatch, dseq, RW_COLS)
            yr, sfin = _rwkv(pr3, state_rwkv_shift[j], _pair_block_diag(state_rwkv[j]), rw, dseq)
            xs = _even_out(xs, of, yr.reshape(dbatch * dseq, RW_WIDTH), wo, ng(layer, 3),
                           ng(layer, 4), ng(layer, 5), win, wout, 2 * layer + 1)
            fk_s.append(k.reshape(dbatch, dseq, FOX_HEADS, HEAD_DIM))
            fv_s.append(v.reshape(dbatch, dseq, FOX_HEADS, HEAD_DIM))
            fl_s.append(lf.reshape(dbatch, dseq, FOX_HEADS))
            rs_s.append(_pair_unblock(sfin))
            rsh_s.append(pr3[:, dseq - 1:, :])
        else:
            xp = _ffn(xp, ng(layer, 0), ng(layer, 1), win, wout, 2 * layer)
            xs = _ffn(xs, ng(layer, 0), ng(layer, 1), win, wout, 2 * layer)
            cpos = jnp.arange(GM_CHUNK) // CHUNK
            wm = jnp.where(cpos[None, :] <= cpos[:, None], gm_w_s[j], 0.0)
            bs = jnp.repeat(gm_b_s[j].T, GM_WIDTH // GM_GROUPS, axis=1)
            reps = GM_CHUNK // dseq
            wm_s = jnp.einsum('ab,gts->gatbs', jnp.eye(reps, dtype=F32), wm[:, :dseq, :dseq])
            wm_s = wm_s.reshape(GM_GROUPS, GM_CHUNK, GM_CHUNK)
            bs_s = jnp.tile(bs[:dseq], (reps, 1))
            odd = (ng(layer, 2), gm_w_in[j].astype(BF16), gm_ln_g[j].reshape(1, GM_WIDTH),
                   gm_ln_b[j].reshape(1, GM_WIDTH))
            tail = (gm_w_out[j].astype(BF16), ng(layer, 3), ng(layer, 4), ng(layer, 5),
                    win, wout, 2 * layer + 1)
            xp, = _odd(xp, *odd, wm.astype(BF16), bs, *tail, emit_v=False)
            xs, gv = _odd(xs, *odd, wm_s.astype(BF16), bs_s, *tail, emit_v=True)
            gv_s.append(gv.reshape(dbatch, dseq, GM_WIDTH))
    return (xp.reshape(batch, seq, D_MODEL), xs.reshape(dbatch, dseq, D_MODEL),
            jnp.stack(fk_p), jnp.stack(fv_p), jnp.stack(fl_p), jnp.stack(rs_p), jnp.stack(rsh_p),
            jnp.stack(fk_s), jnp.stack(fv_s), jnp.stack(fl_s), jnp.stack(rs_s), jnp.stack(rsh_s),
            jnp.stack(gv_s))
```

```python
import functools
import math

import jax
import jax.numpy as jnp
from jax import lax
from jax.experimental import pallas as pl
from jax.experimental.pallas import tpu as pltpu

F32 = jnp.float32
BF16 = jnp.bfloat16

D_MODEL = 1024
D_FF = 2816
HEAD_DIM = 64
FOX_WIDTH = 512
FOX_HEADS = 8
RW_WIDTH = 512
RW_HEADS = 8
RW_COLS = 1792
RW_OFF_W = 1536
RW_OFF_G = 1664
FOX_COLS = 3 * FOX_WIDTH + FOX_HEADS
GM_WIDTH = 1024
GM_CHUNK = 128
GM_GROUPS = 8
CHUNK = 64
FFN_RES_SCALE = 0.5
RMS_EPS = 1e-6
LN_EPS = 1e-5
GN_EPS = 64e-5

LANE = 128
ROW_TILE = 512
FFN_ROW_TILE = 1024
FF_CHUNK = 256
FOX_BLOCK = 256
FOX_VROWS = HEAD_DIM + 16
LOG2E = 1.4426950408889634
RW_CHUNK = 64
RW_ROWS = 1024
RW_GROUP = 2
NEG_BIG = -1e30
VMEM_LIMIT = 56 * 1024 * 1024


def _params(sem):
    return pltpu.CompilerParams(dimension_semantics=sem, vmem_limit_bytes=VMEM_LIMIT)


def _const_spec(shape):
    n = len(shape)
    return pl.BlockSpec(shape, lambda *_: (0,) * n, pipeline_mode=pl.Buffered(1))


def _dot(a, b):
    return jnp.dot(a, b, preferred_element_type=F32)


def _dot_nt(a, b):
    return lax.dot_general(a, b, (((1,), (1,)), ((), ())), preferred_element_type=F32)


def _dot_tn(a, b):
    return lax.dot_general(a, b, (((0,), (0,)), ((), ())), preferred_element_type=F32)


def _split2(x):
    hi = x.astype(BF16)
    lo = (x - hi.astype(F32)).astype(BF16)
    return hi, lo


def _split3(x):
    hi = x.astype(BF16)
    r1 = x - hi.astype(F32)
    mid = r1.astype(BF16)
    lo = (r1 - mid.astype(F32)).astype(BF16)
    return hi, mid, lo


def _rms(x, g):
    return x * lax.rsqrt(jnp.mean(x * x, axis=-1, keepdims=True) + RMS_EPS) * g


def _sigmoid(x):
    return 1.0 / (1.0 + jnp.exp(-x))


def _log_sigmoid(z):
    return jnp.minimum(z, 0.0) - jnp.log1p(jnp.exp(-jnp.abs(z)))


def _ffn_apply(x, gpre, gpost, win_ref, wout_ref, act_ref):
    h = _rms(x, gpre).astype(BF16)
    for c in range(D_FF // FF_CHUNK):
        lo = c * FF_CHUNK
        gate = _dot(h, win_ref[:, lo:lo + FF_CHUNK])
        up = _dot(h, win_ref[:, D_FF + lo:D_FF + lo + FF_CHUNK])
        act_ref[:, lo:lo + FF_CHUNK] = (gate * _sigmoid(gate) * up).astype(BF16)
    y = _dot(act_ref[...], wout_ref[...])
    return x + FFN_RES_SCALE * _rms(y, gpost)


def _ffn_kernel(x_ref, gpre_ref, gpost_ref, win_ref, wout_ref, o_ref, act_ref):
    o_ref[...] = _ffn_apply(x_ref[...], gpre_ref[...], gpost_ref[...], win_ref, wout_ref, act_ref)


def _row_spec(tm, width):
    return pl.BlockSpec((tm, width), lambda i: (i, 0))


def _ffn_weight_specs(widx):
    pick = lambda *_: (widx, 0, 0)
    return [pl.BlockSpec((None, D_MODEL, 2 * D_FF), pick, pipeline_mode=pl.Buffered(1)),
            pl.BlockSpec((None, D_FF, D_MODEL), pick, pipeline_mode=pl.Buffered(1))]


def _ffn(x, gpre, gpost, win, wout, widx):
    n = x.shape[0]
    tm = min(FFN_ROW_TILE, n)
    return pl.pallas_call(
        _ffn_kernel,
        grid=(n // tm,),
        in_specs=[_row_spec(tm, D_MODEL), _const_spec((1, D_MODEL)), _const_spec((1, D_MODEL))]
        + _ffn_weight_specs(widx),
        out_specs=_row_spec(tm, D_MODEL),
        out_shape=jax.ShapeDtypeStruct((n, D_MODEL), F32),
        scratch_shapes=[pltpu.VMEM((tm, D_FF), BF16)],
        compiler_params=_params(("parallel",)),
        name="ffn",
    )(x, gpre, gpost, win, wout)


def _head_major(x):
    per_head = []
    for g in range(x.shape[1] // LANE):
        xg = x[:, g * LANE:(g + 1) * LANE]
        per_head += [xg, pltpu.roll(xg, HEAD_DIM, 1)]
    return pltpu.einshape("htd->thd", jnp.stack(per_head, axis=0))[:, :, :HEAD_DIM]


def _proj_kernel(x_ref, g_ref, wqkv_ref, wf_ref, bf_ref, wrw_ref,
                 q_ref, kb_ref, vb_ref, k_ref, v_ref, lf_ref, lfp_ref, pr_ref, *, time_minor):
    h = _rms(x_ref[...], g_ref[...]).astype(BF16)
    q_ref[...] = _dot(h, wqkv_ref[:, :FOX_WIDTH]).astype(BF16)
    k = _dot(h, wqkv_ref[:, FOX_WIDTH:2 * FOX_WIDTH])
    v = _dot(h, wqkv_ref[:, 2 * FOX_WIDTH:])
    kb_ref[...] = k.astype(BF16)
    vb_ref[...] = v.astype(BF16)
    lf = _log_sigmoid(_dot(h, wf_ref[...]) + bf_ref[...])
    if time_minor:
        k_ref[0] = k.T
        v_ref[0] = v.T
        lf_ref[0] = lf.T[:FOX_HEADS, :]
    else:
        k_ref[...] = _head_major(k)
        v_ref[...] = _head_major(v)
        lf_ref[...] = lf[:, :FOX_HEADS]
    lfp_ref[...] = lf
    pr_ref[...] = _dot(h, wrw_ref[...])


def _proj(x, g, wqkv, wf, bf, wrw, seq=None):
    n = x.shape[0]
    tm = min(ROW_TILE, n)
    if seq is None:
        cache_specs = [pl.BlockSpec((tm, FOX_HEADS, HEAD_DIM), lambda i: (i, 0, 0))] * 2 + [_row_spec(tm, FOX_HEADS)]
        cache_shapes = [jax.ShapeDtypeStruct((n, FOX_HEADS, HEAD_DIM), F32)] * 2 + [
            jax.ShapeDtypeStruct((n, FOX_HEADS), F32)]
    else:
        per = seq // tm
        pick = lambda i: (i // per, 0, i % per)
        cache_specs = [pl.BlockSpec((1, FOX_WIDTH, tm), pick)] * 2 + [pl.BlockSpec((1, FOX_HEADS, tm), pick)]
        cache_shapes = [jax.ShapeDtypeStruct((n // seq, FOX_WIDTH, seq), F32)] * 2 + [
            jax.ShapeDtypeStruct((n // seq, FOX_HEADS, seq), F32)]
    return pl.pallas_call(
        functools.partial(_proj_kernel, time_minor=seq is not None),
        grid=(n // tm,),
        in_specs=[_row_spec(tm, D_MODEL), _const_spec((1, D_MODEL)),
                  _const_spec((D_MODEL, 3 * FOX_WIDTH)), _const_spec((D_MODEL, LANE)),
                  _const_spec((1, LANE)), _const_spec((D_MODEL, RW_COLS))],
        out_specs=[_row_spec(tm, FOX_WIDTH), _row_spec(tm, FOX_WIDTH), _row_spec(tm, FOX_WIDTH)]
        + cache_specs + [_row_spec(tm, LANE), _row_spec(tm, RW_COLS)],
        out_shape=[jax.ShapeDtypeStruct((n, FOX_WIDTH), BF16)] * 3 + cache_shapes + [
            jax.ShapeDtypeStruct((n, LANE), F32), jax.ShapeDtypeStruct((n, RW_COLS), F32)],
        compiler_params=_params(("parallel",)),
        name="even_proj",
    )(x, g, wqkv, wf, bf, wrw)


def _lane_cumsum(rows, carry):
    r = lax.broadcasted_iota(jnp.int32, (LANE, LANE), 0)
    c = lax.broadcasted_iota(jnp.int32, (LANE, LANE), 1)
    upper = (r <= c).astype(BF16)
    hi, mid, lo = _split3(rows)
    return _dot(hi, upper) + _dot(mid, upper) + _dot(lo, upper) + carry


def _fox_prompt_kernel(q_ref, k_ref, v_ref, lfp_ref, o_ref, kx_ref, vt_ref, qx_ref, m_ref, alpha_ref, p_ref,
                       acc_ref, *, seq):
    i = pl.program_id(1)
    bq = FOX_BLOCK
    n_pairs = FOX_HEADS // 2
    heads = range(FOX_HEADS)
    val_rows = [slice(h * FOX_VROWS, h * FOX_VROWS + HEAD_DIM) for h in heads]
    ext_rows = [slice(h * FOX_VROWS, (h + 1) * FOX_VROWS) for h in heads]

    @pl.when(i == 0)
    def _():
        r = lax.broadcasted_iota(jnp.int32, (LANE, LANE), 0)
        c = lax.broadcasted_iota(jnp.int32, (LANE, LANE), 1)
        tri = (c <= r).astype(BF16)
        ones_row = (r[:FOX_VROWS - HEAD_DIM] == 0).astype(BF16)
        carry = jnp.zeros((1, LANE), F32)
        for blk in range(seq // LANE):
            rows = slice(blk * LANE, (blk + 1) * LANE)
            hi, mid, lo = _split3(jnp.where(c < FOX_HEADS, lfp_ref[rows, :], 0.0))
            cum = _dot(tri, hi) + _dot(tri, mid) + _dot(tri, lo) + carry
            carry = cum[LANE - 1:LANE, :]
            chi, cmid, clo = _split3(cum * LOG2E)
            feat = (chi.astype(F32) + pltpu.roll(cmid.astype(F32), FOX_HEADS, 1)
                    + pltpu.roll(clo.astype(F32), 2 * FOX_HEADS, 1)).astype(BF16)
            for p in range(n_pairs):
                lanes = slice(p * LANE, (p + 1) * LANE)
                kx_ref[p, rows, :LANE] = k_ref[rows, lanes]
                kx_ref[p, rows, LANE:] = feat
                vt = v_ref[rows, lanes].astype(F32).T.astype(BF16)
                vt_ref[val_rows[2 * p], rows] = vt[:HEAD_DIM]
                vt_ref[val_rows[2 * p + 1], rows] = vt[HEAD_DIM:]
            for h in heads:
                vt_ref[h * FOX_VROWS + HEAD_DIM:(h + 1) * FOX_VROWS, rows] = ones_row

    q0 = pl.multiple_of(i * bq, bq)
    lane = lax.broadcasted_iota(jnp.int32, (1, LANE), 1)
    krow = lax.broadcasted_iota(jnp.int32, (bq, bq), 0)
    qcol = lax.broadcasted_iota(jnp.int32, (bq, bq), 1)
    causal = krow <= qcol
    feat_row = lax.broadcasted_iota(jnp.int32, (LANE, 1), 0)
    for p in range(n_pairs):
        qpt = (q_ref[:, p * LANE:(p + 1) * LANE].astype(F32) * (HEAD_DIM ** -0.5 * LOG2E)).T
        for hh in range(2):
            h = 2 * p + hh
            in_head = (feat_row >= HEAD_DIM) if hh else (feat_row < HEAD_DIM)
            pick = (feat_row == h) | (feat_row == h + FOX_HEADS) | (feat_row == h + 2 * FOX_HEADS)
            qx_ref[h, :LANE, :] = jnp.where(in_head, qpt, 0.0).astype(BF16)
            qx_ref[h, LANE:, :] = jnp.broadcast_to(jnp.where(pick, -1.0, 0.0), (LANE, bq)).astype(BF16)

    def scores(j0):
        return [_dot(kx_ref[h // 2, pl.ds(j0, bq), :], qx_ref[h]) for h in heads]

    def softmax(sts, diag):
        for h in heads:
            st = jnp.where(causal, sts[h], NEG_BIG) if diag else sts[h]
            m_old = m_ref[h:h + 1, :]
            m_new = jnp.maximum(m_old, jnp.max(st, axis=0, keepdims=True))
            alpha_ref[h:h + 1, :] = jnp.exp2(m_old - m_new)
            m_ref[h:h + 1, :] = m_new
            p_ref[h] = jnp.exp2(st - m_new).astype(BF16)

    def accumulate(j0):
        pvs = [_dot(vt_ref[ext_rows[h], pl.ds(j0, bq)], p_ref[h]) for h in heads]
        for h in heads:
            acc_ref[ext_rows[h], :] = alpha_ref[h:h + 1, :] * acc_ref[ext_rows[h], :] + pvs[h]

    m_ref[...] = jnp.full(m_ref.shape, NEG_BIG, F32)
    acc_ref[...] = jnp.zeros(acc_ref.shape, F32)
    softmax(scores(q0), True)

    def body(j, _):
        j0 = pl.multiple_of(j * bq, bq)
        sts = scores(j0)
        accumulate(pl.multiple_of(jnp.where(j == 0, q0, j0 - bq), bq))
        softmax(sts, False)
        return 0

    lax.fori_loop(0, i, body, 0)
    accumulate(pl.multiple_of(jnp.where(i == 0, q0, q0 - bq), bq))
    out = [acc_ref[val_rows[h], :] * (1.0 / acc_ref[h * FOX_VROWS + HEAD_DIM:h * FOX_VROWS + HEAD_DIM + 1, :])
           for h in heads]
    o_ref[...] = jnp.concatenate(out, axis=0).T.astype(BF16)


def _fox_prompt(q, k, v, lfp, batch, seq):
    nq = seq // FOX_BLOCK
    return pl.pallas_call(
        functools.partial(_fox_prompt_kernel, seq=seq),
        grid=(batch, nq),
        in_specs=[pl.BlockSpec((FOX_BLOCK, FOX_WIDTH), lambda b, i: (b * nq + i, 0)),
                  pl.BlockSpec((seq, FOX_WIDTH), lambda b, i: (b, 0)),
                  pl.BlockSpec((seq, FOX_WIDTH), lambda b, i: (b, 0)),
                  pl.BlockSpec((seq, LANE), lambda b, i: (b, 0))],
        out_specs=pl.BlockSpec((FOX_BLOCK, FOX_WIDTH), lambda b, i: (b * nq + i, 0)),
        out_shape=jax.ShapeDtypeStruct((batch * seq, FOX_WIDTH), BF16),
        scratch_shapes=[pltpu.VMEM((FOX_HEADS // 2, seq, 2 * LANE), BF16),
                        pltpu.VMEM((FOX_HEADS * FOX_VROWS, seq), BF16),
                        pltpu.VMEM((FOX_HEADS, FOX_BLOCK, 2 * LANE), BF16),
                        pltpu.VMEM((FOX_HEADS, FOX_BLOCK), F32),
                        pltpu.VMEM((FOX_HEADS, FOX_BLOCK), F32),
                        pltpu.VMEM((FOX_HEADS, FOX_BLOCK, FOX_BLOCK), BF16),
                        pltpu.VMEM((FOX_HEADS * FOX_VROWS, FOX_BLOCK), F32)],
        compiler_params=_params(("parallel", "arbitrary")),
        name="fox_prompt",
    )(q, k, v, lfp)


def _fox_sample_kernel(q_ref, kn_ref, vn_ref, lfn_ref, kc_ref, vc_ref, lfc_ref, o_ref,
                       kb_ref, vb_ref, *, past, tnew):
    heads = range(FOX_HEADS)
    hrows = [slice(h * HEAD_DIM, (h + 1) * HEAD_DIM) for h in heads]
    for h in heads:
        kb_ref[hrows[h], :] = kc_ref[0, h].astype(BF16)
        vb_ref[hrows[h], :] = vc_ref[0, h].astype(BF16)
    q = q_ref[...].astype(F32) * (HEAD_DIM ** -0.5)
    qh = [q[:, hrows[h]].astype(BF16) for h in heads]

    carry = jnp.zeros((FOX_HEADS, 1), F32)
    cts = []
    for b in range(past // LANE):
        c = _lane_cumsum(lfc_ref[0, :, b * LANE:(b + 1) * LANE], carry)
        cts.append(c)
        carry = c[:, LANE - 1:LANE]
    total = carry
    r = lax.broadcasted_iota(jnp.int32, (tnew, tnew), 0)
    c = lax.broadcasted_iota(jnp.int32, (tnew, tnew), 1)
    hi, mid, lo = _split3(lfn_ref[0])
    upper = (r <= c).astype(BF16)
    cnew = _dot(hi, upper) + _dot(mid, upper) + _dot(lo, upper)

    ccache = jnp.concatenate(cts, axis=1) - total
    kn, vn = kn_ref[...], vn_ref[...]
    sc = [_dot(qh[h], kb_ref[hrows[h], :]) - ccache[h:h + 1, :] for h in heads]
    sn = [jnp.where(c <= r, _dot_nt(qh[h], kn[:, hrows[h]]) - cnew[h:h + 1, :], NEG_BIG) for h in heads]
    pcs, pns, ls = [], [], []
    for h in heads:
        m = jnp.maximum(jnp.max(sc[h], axis=1, keepdims=True), jnp.max(sn[h], axis=1, keepdims=True))
        pc = jnp.exp(sc[h] - m)
        pn = jnp.exp(sn[h] - m)
        ls.append(jnp.sum(pc, axis=1, keepdims=True) + jnp.sum(pn, axis=1, keepdims=True))
        pcs.append(pc.astype(BF16))
        pns.append(pn.astype(BF16))
    outs = [(_dot_nt(pcs[h], vb_ref[hrows[h], :]) + _dot(pns[h], vn[:, hrows[h]])) / ls[h] for h in heads]
    o_ref[...] = jnp.concatenate(outs, axis=1).astype(BF16)


def _fox_sample(q, kn, vn, lfn_t, kc, vc, lfc_t, batch, tnew, past):
    return pl.pallas_call(
        functools.partial(_fox_sample_kernel, past=past, tnew=tnew),
        grid=(batch,),
        in_specs=[pl.BlockSpec((tnew, FOX_WIDTH), lambda b: (b, 0)),
                  pl.BlockSpec((tnew, FOX_WIDTH), lambda b: (b, 0)),
                  pl.BlockSpec((tnew, FOX_WIDTH), lambda b: (b, 0)),
                  pl.BlockSpec((1, FOX_HEADS, tnew), lambda b: (b, 0, 0)),
                  pl.BlockSpec((1, FOX_HEADS, HEAD_DIM, past), lambda b: (b, 0, 0, 0)),
                  pl.BlockSpec((1, FOX_HEADS, HEAD_DIM, past), lambda b: (b, 0, 0, 0)),
                  pl.BlockSpec((1, FOX_HEADS, past), lambda b: (b, 0, 0))],
        out_specs=pl.BlockSpec((tnew, FOX_WIDTH), lambda b: (b, 0)),
        out_shape=jax.ShapeDtypeStruct((batch * tnew, FOX_WIDTH), BF16),
        scratch_shapes=[pltpu.VMEM((FOX_WIDTH, past), BF16), pltpu.VMEM((FOX_WIDTH, past), BF16)],
        compiler_params=_params(("parallel",)),
        name="fox_sample",
    )(q, kn, vn, lfn_t, kc, vc, lfc_t)


def _seg_sum(z, e_ref):
    zb = z.astype(BF16)
    rows, groups = z.shape[0], z.shape[1] // LANE
    s = _dot(jnp.concatenate([zb[:, p * LANE:(p + 1) * LANE] for p in range(groups)], axis=0), e_ref[...])
    return jnp.concatenate([s[p * rows:(p + 1) * rows] for p in range(groups)], axis=1)


def _rwkv_kernel(pr_ref, prev_ref, s0_ref, mu_ref, w0_ref, w2h_ref, w2l_ref, a0_ref, a2_ref, g2_ref,
                 kk_ref, ka_ref, rk_ref, lng_ref, lnb_ref, e_ref,
                 y_ref, sfin_ref, s_ref, pv_ref, *, n_chunks, t_valid):
    cc = RW_CHUNK
    t = pl.program_id(1)

    @pl.when(t == 0)
    def _():
        s_ref[...] = s0_ref[0]
        pv_ref[...] = prev_ref[0]

    rows = n_chunks * cc
    gsize = min(RW_GROUP, n_chunks)
    groups = [list(range(c0, c0 + gsize)) for c0 in range(0, n_chunks, gsize)]
    gr = gsize * cc
    ri = lax.broadcasted_iota(jnp.int32, (gr, gr), 0)
    ci = lax.broadcasted_iota(jnp.int32, (gr, gr), 1)
    tri = ((ci <= ri) & (ci // cc == ri // cc)).astype(BF16)
    r2 = lax.broadcasted_iota(jnp.int32, (2 * cc, 2 * cc), 0) % cc
    c2 = lax.broadcasted_iota(jnp.int32, (2 * cc, 2 * cc), 1) % cc
    strict = c2 < r2
    incl = c2 <= r2
    row1 = lax.broadcasted_iota(jnp.int32, (rows, 1), 0)

    pr = pr_ref[0]
    n_real = pr.shape[0]
    if n_real < rows:
        pr = jnp.concatenate([pr, jnp.zeros((rows - n_real, RW_COLS), F32)], axis=0)
    prev_seq = jnp.where(row1 == 0, pv_ref[...], pltpu.roll(pr, 1, 0))
    pv_ref[...] = pr[n_real - 1:n_real, :]
    xm = pr + (prev_seq - pr) * mu_ref[...]
    tok = {}
    cmids = {}

    xwa = xm[:, RW_OFF_W:RW_OFF_G]
    th, tl = _split2(jnp.tanh(xwa))
    zw_all = w0_ref[...] + (_dot(th, w2h_ref[...]) + _dot(th, w2l_ref[...]) + _dot(tl, w2h_ref[...]))
    za_all = a0_ref[...] + _dot(xwa.astype(BF16), a2_ref[...])
    g_all = _dot(_sigmoid(xm[:, RW_OFF_G:]).astype(BF16), g2_ref[...])

    def tw_project(chunks):
        rs = slice(chunks[0] * cc, (chunks[-1] + 1) * cc)
        x = xm[rs]
        tok[chunks[0]] = dict(r=x[:, :RW_WIDTH], kx=x[:, RW_WIDTH:2 * RW_WIDTH], vx=x[:, 2 * RW_WIDTH:RW_OFF_W],
                              zw=zw_all[rs], za=za_all[rs], g=g_all[rs])

    def tw_decay(chunks):
        d = tok[chunks[0]]
        lw = -math.exp(-0.5) * _sigmoid(d.pop("zw"))
        asig = _sigmoid(d.pop("za"))
        kkv = d["kx"] * kk_ref[...]
        kkn = kkv / jnp.maximum(jnp.sqrt(_seg_sum(kkv * kkv, e_ref)), 1e-12)
        d["kmod"] = d.pop("kx") * (1.0 + (asig - 1.0) * ka_ref[...])
        d["a_"], d["b_"], d["k_"] = -kkn, kkn * asig, d["kmod"]
        if t_valid < rows:
            valid = row1[chunks[0] * cc:(chunks[-1] + 1) * cc] < t_valid
            lw = jnp.where(valid, lw, 0.0)
            d["b_"] = jnp.where(valid, d["b_"], 0.0)
            d["k_"] = jnp.where(valid, d["k_"], 0.0)
        hi, mid, lo = _split3(lw)
        d["lw"] = lw
        d["cum"] = _dot(tri, hi) + _dot(tri, mid) + _dot(tri, lo)

    def tw_scale(chunks):
        d = tok[chunks[0]]
        cum = d["cum"]
        for i, c in enumerate(chunks):
            cmids[c] = cum[i * cc + cc // 2 - 1:i * cc + cc // 2, :]
        cmid = jnp.concatenate([jnp.broadcast_to(cmids[c], (cc, RW_WIDTH)) for c in chunks], axis=0)
        d["at"] = (d.pop("a_") * jnp.exp(cum - d.pop("lw") - cmid)).astype(BF16)
        d["rt"] = (d["r"] * jnp.exp(cum - cmid)).astype(BF16)
        e_neg = jnp.exp(cmid - cum)
        d["bt"] = (d.pop("b_") * e_neg).astype(BF16)
        d["kt"] = (d.pop("k_") * e_neg).astype(BF16)
        d["vb"] = d["vx"].astype(BF16)

    def token_stages(chunks):
        return [functools.partial(f, chunks) for f in (tw_project, tw_decay, tw_scale)]

    lane = lax.broadcasted_iota(jnp.int32, (cc, LANE), 1)
    zero = jnp.zeros((cc, LANE), BF16)

    def stack(x):
        return jnp.concatenate([jnp.where(lane < HEAD_DIM, x, zero), jnp.where(lane >= HEAD_DIM, x, zero)], axis=0)

    n2 = 2 * cc
    n_pairs = RW_HEADS // 2
    pairs = range(n_pairs)
    eye = (lax.broadcasted_iota(jnp.int32, (n2, n2), 0) == lax.broadcasted_iota(jnp.int32, (n2, n2), 1)).astype(F32)
    dat = {(c, p): {} for c in range(n_chunks) for p in pairs}

    def group_of(c):
        return c - c % gsize

    def st_scores(cps):
        for c, p in cps:
            tk = tok[group_of(c)]
            off = (c - group_of(c)) * cc
            rs, lanes = slice(off, off + cc), slice(p * LANE, (p + 1) * LANE)
            d = dat[c, p]
            d["a"], d["r"], b, k, d["v"] = [
                stack(tk[name][rs, lanes]) for name in ("at", "rt", "bt", "kt", "vb")]
            d["bk"] = jnp.concatenate([b, k], axis=0)
        for cp in cps:
            d = dat[cp]
            d["m"] = _dot_nt(jnp.concatenate([d["a"], d["r"]], axis=0), d["bk"])

    def st_local(cps):
        for cp in cps:
            d = dat[cp]
            m = d.pop("m")
            d["l"] = jnp.where(strict, m[:n2, :n2], 0.0)
            d["mrb"] = jnp.where(incl, m[n2:, :n2], 0.0).astype(BF16)
            mk = jnp.concatenate([jnp.where(strict, m[:n2, n2:], 0.0), jnp.where(incl, m[n2:, n2:], 0.0)], axis=0)
            mv = _dot(mk.astype(BF16), d["v"])
            d["mv"], d["y0"] = mv[:n2].astype(BF16), mv[n2:]
        for cp in cps:
            d = dat[cp]
            lb = d["l"].astype(BF16)
            d["t"] = eye + d["l"]
            d["l"] = _dot(lb, lb)

    def st_double(cps, last):
        for cp in cps:
            d = dat[cp]
            lb, tb = d["l"].astype(BF16), d["t"].astype(BF16)
            if last:
                d["t"] = d["t"] + _dot(lb, tb)
            else:
                z = _dot(lb, jnp.concatenate([lb, tb], axis=1))
                d["l"], d["t"] = z[:, :n2], d["t"] + z[:, n2:]

    def st_solve(cps):
        for cp in cps:
            d = dat[cp]
            wu = _dot(d["t"].astype(BF16), jnp.concatenate([d["a"], d["mv"]], axis=1))
            d["wr"] = jnp.concatenate([wu[:, :LANE].astype(BF16), d["r"]], axis=0)
            d["u0"] = wu[:, LANE:]
            del d["l"], d["t"], d["a"], d["mv"]

    def local_stages(cps):
        n_double = int(math.log2(cc)) - 1
        return ([lambda: st_scores(cps), lambda: st_local(cps)]
                + [functools.partial(st_double, cps, s_i + 1 == n_double) for s_i in range(n_double)]
                + [lambda: st_solve(cps)])

    state = [s_ref[p] for p in pairs]
    ychunks = {}

    def ch_enter(c):
        g_mid = jnp.exp(cmids[c])
        for p in pairs:
            d = dat[c, p]
            d["sp"] = state[p] * g_mid[:, p * LANE:(p + 1) * LANE]
            d["ws"] = _dot_nt(d["wr"], d["sp"].astype(BF16))

    def ch_leave(c):
        off = (c - group_of(c)) * cc
        g_end = jnp.exp(tok[group_of(c)]["cum"][off + cc - 1:off + cc, :] - cmids[c])
        y2 = []
        for p in pairs:
            d = dat[c, p]
            d["ub"] = (d["ws"][:n2] + d["u0"]).astype(BF16)
        for p in pairs:
            d = dat[c, p]
            y2.append(d["ws"][n2:] + _dot(d["mrb"], d["ub"]) + d["y0"])
            upd = _dot_tn(jnp.concatenate([d["ub"], d["v"]], axis=0), d["bk"])
            state[p] = (d["sp"] + upd) * g_end[:, p * LANE:(p + 1) * LANE]
            dat[c, p] = None
        ychunks[c] = jnp.concatenate([t[:cc] + t[cc:] for t in y2], axis=1)

    def finish(chunks):
        tk = tok[chunks[0]]
        y = jnp.concatenate([ychunks[c] for c in chunks], axis=0) if len(chunks) > 1 else ychunks[chunks[0]]
        ym = _seg_sum(y, e_ref) * (1.0 / HEAD_DIM)
        yc = y - ym
        yv = _seg_sum(yc * yc, e_ref) * (1.0 / HEAD_DIM)
        yn = yc * lax.rsqrt(yv + GN_EPS) * lng_ref[...] + lnb_ref[...]
        bonus = _seg_sum(tk["r"] * tk["kmod"] * rk_ref[...], e_ref) * tk["vx"]
        lo = chunks[0] * cc
        hi = min((chunks[-1] + 1) * cc, n_real)
        y_ref[0, lo:hi, :] = ((yn + bonus) * tk["g"]).astype(BF16)[:hi - lo]

    def emit(main, side):
        for stage in main:
            stage()
            if side:
                side.pop(0)()
        while side:
            side.pop(0)()

    def serial_stages(chunks):
        fs = [f for c in chunks for f in (functools.partial(ch_enter, c), functools.partial(ch_leave, c))]
        return fs + [functools.partial(finish, chunks)]

    for stage in token_stages(groups[0]):
        stage()
    for gi, chunks in enumerate(groups):
        nxt = token_stages(groups[gi + 1]) if gi + 1 < len(groups) else []
        prv = serial_stages(groups[gi - 1]) if gi else []
        side = [f for pair in zip(nxt, prv) for f in pair] + nxt[len(prv):] + prv[len(nxt):]
        emit(local_stages([(c, p) for c in chunks for p in pairs]), side)
    for stage in serial_stages(groups[-1]):
        stage()
    for p in pairs:
        s_ref[p] = state[p]

    @pl.when(t == pl.num_programs(1) - 1)
    def _():
        sfin_ref[0] = s_ref[...]


def _rwkv(pr, prev, s0, w, t_valid):
    batch, seq, _ = pr.shape
    rows = min(RW_ROWS, seq)
    n_pairs = RW_HEADS // 2
    vec = lambda n: _const_spec((1, n))
    return pl.pallas_call(
        functools.partial(_rwkv_kernel, n_chunks=pl.cdiv(rows, RW_CHUNK), t_valid=t_valid),
        grid=(batch, seq // rows),
        in_specs=[pl.BlockSpec((1, rows, RW_COLS), lambda b, t: (b, t, 0)),
                  pl.BlockSpec((1, 1, RW_COLS), lambda b, t: (b, 0, 0)),
                  pl.BlockSpec((1, n_pairs, LANE, LANE), lambda b, t: (b, 0, 0, 0)),
                  vec(RW_COLS), vec(RW_WIDTH), _const_spec((LANE, RW_WIDTH)), _const_spec((LANE, RW_WIDTH)),
                  vec(RW_WIDTH), _const_spec((LANE, RW_WIDTH)), _const_spec((LANE, RW_WIDTH)),
                  vec(RW_WIDTH), vec(RW_WIDTH), vec(RW_WIDTH), vec(RW_WIDTH), vec(RW_WIDTH),
                  _const_spec((LANE, LANE))],
        out_specs=[pl.BlockSpec((1, rows, RW_WIDTH), lambda b, t: (b, t, 0)),
                   pl.BlockSpec((1, n_pairs, LANE, LANE), lambda b, t: (b, 0, 0, 0))],
        out_shape=[jax.ShapeDtypeStruct((batch, seq, RW_WIDTH), BF16),
                   jax.ShapeDtypeStruct((batch, n_pairs, LANE, LANE), F32)],
        scratch_shapes=[pltpu.VMEM((n_pairs, LANE, LANE), F32), pltpu.VMEM((1, RW_COLS), F32)],
        compiler_params=_params(("parallel", "arbitrary")),
        name="rwkv7",
    )(pr, prev, s0, w["mu"], w["w0"], w["w2h"], w["w2l"], w["a0"], w["a2"], w["g2"],
      w["kk"], w["ka"], w["rk"], w["ln_g"], w["ln_b"], w["seg"])


def _even_out_kernel(x_ref, of_ref, yr_ref, wo_ref, g3_ref, gpre_ref, gpost_ref, win_ref, wout_ref,
                     o_ref, act_ref):
    mixed = _dot(of_ref[...], wo_ref[:FOX_WIDTH, :]) + _dot(yr_ref[...], wo_ref[FOX_WIDTH:, :])
    x = x_ref[...] + _rms(mixed, g3_ref[...])
    o_ref[...] = _ffn_apply(x, gpre_ref[...], gpost_ref[...], win_ref, wout_ref, act_ref)


def _even_out(x, of, yr, wo, g3, gpre, gpost, win, wout, widx):
    n = x.shape[0]
    tm = min(FFN_ROW_TILE, n)
    vec = _const_spec((1, D_MODEL))
    return pl.pallas_call(
        _even_out_kernel,
        grid=(n // tm,),
        in_specs=[_row_spec(tm, D_MODEL), _row_spec(tm, FOX_WIDTH), _row_spec(tm, RW_WIDTH),
                  _const_spec((D_MODEL, D_MODEL)), vec, vec, vec] + _ffn_weight_specs(widx),
        out_specs=_row_spec(tm, D_MODEL),
        out_shape=jax.ShapeDtypeStruct((n, D_MODEL), F32),
        scratch_shapes=[pltpu.VMEM((tm, D_FF), BF16)],
        compiler_params=_params(("parallel",)),
        name="even_out_ffn",
    )(x, of, yr, wo, g3, gpre, gpost, win, wout)


def _gelu(x):
    return 0.5 * x * (1.0 + lax.erf(x * (2.0 ** -0.5)))


def _odd_kernel(x_ref, g2_ref, win_ref, lng_ref, lnb_ref, wm_ref, bs_ref, wo_ref, g3_ref,
                gpre_ref, gpost_ref, fwin_ref, fwout_ref, o_ref, *rest):
    gv_ref = rest[0] if len(rest) == 3 else None
    act_ref, gate_ref = rest[-2:]
    x = x_ref[...]
    tm = x.shape[0]
    h = _rms(x, g2_ref[...]).astype(BF16)
    v = _gelu(_dot(h, win_ref[:, GM_WIDTH:]))
    vm = jnp.mean(v, axis=-1, keepdims=True)
    vc = v - vm
    var = jnp.mean(vc * vc, axis=-1, keepdims=True)
    v = vc * lax.rsqrt(var + LN_EPS) * lng_ref[...] + lnb_ref[...]
    if gv_ref is not None:
        gv_ref[...] = v
    vb = v.astype(BF16)
    u = _gelu(_dot(h, win_ref[:, :GM_WIDTH]))
    for c in range(tm // GM_CHUNK):
        rows = slice(c * GM_CHUNK, (c + 1) * GM_CHUNK)
        sp = jnp.concatenate(
            [_dot(wm_ref[g], vb[rows, g * LANE:(g + 1) * LANE]) for g in range(GM_GROUPS)], axis=1)
        gate_ref[rows, :] = (u[rows, :] * (sp + bs_ref[...])).astype(BF16)
    mixed = _dot(gate_ref[...], wo_ref[...])
    x = x + _rms(mixed, g3_ref[...])
    o_ref[...] = _ffn_apply(x, gpre_ref[...], gpost_ref[...], fwin_ref, fwout_ref, act_ref)


def _odd(x, g2, win, lng, lnb, wm, bs, wo, g3, gpre, gpost, fwin, fwout, widx, emit_v):
    n = x.shape[0]
    tm = min(ROW_TILE, n)
    vec = _const_spec((1, D_MODEL))
    return pl.pallas_call(
        _odd_kernel,
        grid=(n // tm,),
        in_specs=[_row_spec(tm, D_MODEL), vec, _const_spec((D_MODEL, 2 * GM_WIDTH)), vec, vec,
                  _const_spec((GM_GROUPS, GM_CHUNK, GM_CHUNK)), _const_spec((GM_CHUNK, GM_WIDTH)),
                  _const_spec((GM_WIDTH, D_MODEL)), vec, vec, vec] + _ffn_weight_specs(widx),
        out_specs=[_row_spec(tm, D_MODEL)] + [_row_spec(tm, GM_WIDTH)] * emit_v,
        out_shape=[jax.ShapeDtypeStruct((n, D_MODEL), F32)] + [jax.ShapeDtypeStruct((n, GM_WIDTH), F32)] * emit_v,
        scratch_shapes=[pltpu.VMEM((tm, D_FF), BF16), pltpu.VMEM((tm, GM_WIDTH), BF16)],
        compiler_params=_params(("parallel",)),
        name="odd_mixer_ffn",
    )(x, g2, win, lng, lnb, wm, bs, wo, g3, gpre, gpost, fwin, fwout)


def _pair_block_diag(s):
    b = s.shape[0]
    s = s.reshape(b, RW_HEADS // 2, 2, HEAD_DIM, HEAD_DIM)
    z = jnp.zeros_like(s[:, :, 0])
    top = jnp.concatenate([s[:, :, 0], z], axis=-1)
    bot = jnp.concatenate([z, s[:, :, 1]], axis=-1)
    return jnp.concatenate([top, bot], axis=-2)


def _pair_unblock(s):
    b = s.shape[0]
    h0 = s[:, :, :HEAD_DIM, :HEAD_DIM]
    h1 = s[:, :, HEAD_DIM:, HEAD_DIM:]
    return jnp.stack([h0, h1], axis=2).reshape(b, RW_HEADS, HEAD_DIM, HEAD_DIM)


def kernel(x_prompt, x_sample, cache_fox_k, cache_fox_v, cache_fox_logf, state_rwkv, state_rwkv_shift,
           norm_g, ffn_w_in, ffn_w_out, even_w_in, fox_bf, rw_mu, rw_w0, rw_w2, rw_a0, rw_a2, rw_g2,
           rw_kk, rw_ka, rw_rk, rw_ln_g, rw_ln_b, even_w_out, gm_w_in, gm_ln_g, gm_ln_b, gm_w_s,
           gm_b_s, gm_w_out):
    batch, seq, _ = x_prompt.shape
    dbatch, dseq, _ = x_sample.shape
    past = cache_fox_k.shape[2]
    depth = norm_g.shape[0]
    xp = x_prompt.reshape(batch * seq, D_MODEL)
    xs = x_sample.reshape(dbatch * dseq, D_MODEL)
    ng = lambda layer, i: norm_g[layer, i].reshape(1, D_MODEL)
    win = ffn_w_in.astype(BF16).reshape(depth * 2, D_MODEL, 2 * D_FF)
    wout = ffn_w_out.astype(BF16).reshape(depth * 2, D_FF, D_MODEL)

    fk_p, fv_p, fl_p, rs_p, rsh_p = [], [], [], [], []
    fk_s, fv_s, fl_s, rs_s, rsh_s = [], [], [], [], []
    gv_s = []
    for layer in range(depth):
        j = layer // 2
        if layer % 2 == 0:
            xp = _ffn(xp, ng(layer, 0), ng(layer, 1), win, wout, 2 * layer)
            xs = _ffn(xs, ng(layer, 0), ng(layer, 1), win, wout, 2 * layer)
            ew = even_w_in[j]
            wqkv = ew[:, :3 * FOX_WIDTH].astype(BF16)
            wf = jnp.pad(ew[:, 3 * FOX_WIDTH:FOX_COLS], ((0, 0), (0, LANE - FOX_HEADS))).astype(BF16)
            bf = jnp.pad(fox_bf[j], (0, LANE - FOX_HEADS)).reshape(1, LANE)
            wrw = ew[:, FOX_COLS:].astype(BF16)
            w2 = jnp.pad(rw_w2[j], ((0, LANE - rw_w2.shape[1]), (0, 0)))
            w2h = w2.astype(BF16)
            head_of = jnp.arange(LANE) // HEAD_DIM
            rw = dict(
                mu=rw_mu[j].reshape(1, RW_COLS), w0=rw_w0[j].reshape(1, RW_WIDTH),
                w2h=w2h, w2l=(w2 - w2h.astype(F32)).astype(BF16),
                a0=rw_a0[j].reshape(1, RW_WIDTH),
                a2=jnp.pad(rw_a2[j], ((LANE - rw_a2.shape[1], 0), (0, 0))).astype(BF16),
                g2=rw_g2[j].astype(BF16),
                kk=rw_kk[j].reshape(1, RW_WIDTH), ka=rw_ka[j].reshape(1, RW_WIDTH),
                rk=rw_rk[j].reshape(1, RW_WIDTH),
                ln_g=rw_ln_g[j].reshape(1, RW_WIDTH), ln_b=rw_ln_b[j].reshape(1, RW_WIDTH),
                seg=(head_of[:, None] == head_of[None, :]).astype(BF16))
            wo = even_w_out[j].astype(BF16)

            q, kb, vb, k, v, lf, lfp, pr = _proj(xp, ng(layer, 2), wqkv, wf, bf, wrw, seq=seq)
            of = _fox_prompt(q, kb, vb, lfp, batch, seq)
            pr3 = pr.reshape(batch, seq, RW_COLS)
            yr, sfin = _rwkv(pr3, jnp.zeros((batch, 1, RW_COLS), F32),
                             jnp.zeros((batch, RW_HEADS // 2, LANE, LANE), F32), rw, seq)
            xp = _even_out(xp, of, yr.reshape(batch * seq, RW_WIDTH), wo, ng(layer, 3),
                           ng(layer, 4), ng(layer, 5), win, wout, 2 * layer + 1)
            fk_p.append(jnp.transpose(k.reshape(batch, FOX_HEADS, HEAD_DIM, seq), (0, 3, 1, 2)))
            fv_p.append(jnp.transpose(v.reshape(batch, FOX_HEADS, HEAD_DIM, seq), (0, 3, 1, 2)))
            fl_p.append(jnp.transpose(lf, (0, 2, 1)))
            rs_p.append(_pair_unblock(sfin))
            rsh_p.append(pr3[:, seq - 1:, :])

            q, kb, vb, k, v, lf, _, pr = _proj(xs, ng(layer, 2), wqkv, wf, bf, wrw)
            lfn_t = jnp.swapaxes(lf.reshape(dbatch, dseq, FOX_HEADS), 1, 2)
            lfc_t = jnp.swapaxes(cache_fox_logf[j], 1, 2)
            of = _fox_sample(q, kb, vb, lfn_t, jnp.transpose(cache_fox_k[j], (0, 2, 3, 1)),
                             jnp.transpose(cache_fox_v[j], (0, 2, 3, 1)), lfc_t, dbatch, dseq, past)
            pr3 = pr.reshape(dbatch, dseq, RW_COLS)
            yr, sfin = _rwkv(pr3, state_rwkv_shift[j], _pair_block_diag(state_rwkv[j]), rw, dseq)
            xs = _even_out(xs, of, yr.reshape(dbatch * dseq, RW_WIDTH), wo, ng(layer, 3),
                           ng(layer, 4), ng(layer, 5), win, wout, 2 * layer + 1)
            fk_s.append(k.reshape(dbatch, dseq, FOX_HEADS, HEAD_DIM))
            fv_s.append(v.reshape(dbatch, dseq, FOX_HEADS, HEAD_DIM))
            fl_s.append(lf.reshape(dbatch, dseq, FOX_HEADS))
            rs_s.append(_pair_unblock(sfin))
            rsh_s.append(pr3[:, dseq - 1:, :])
        else:
            xp = _ffn(xp, ng(layer, 0), ng(layer, 1), win, wout, 2 * layer)
            xs = _ffn(xs, ng(layer, 0), ng(layer, 1), win, wout, 2 * layer)
            cpos = jnp.arange(GM_CHUNK) // CHUNK
            wm = jnp.where(cpos[None, :] <= cpos[:, None], gm_w_s[j], 0.0)
            bs = jnp.repeat(gm_b_s[j].T, GM_WIDTH // GM_GROUPS, axis=1)
            reps = GM_CHUNK // dseq
            wm_s = jnp.einsum('ab,gts->gatbs', jnp.eye(reps, dtype=F32), wm[:, :dseq, :dseq])
            wm_s = wm_s.reshape(GM_GROUPS, GM_CHUNK, GM_CHUNK)
            bs_s = jnp.tile(bs[:dseq], (reps, 1))
            odd = (ng(layer, 2), gm_w_in[j].astype(BF16), gm_ln_g[j].reshape(1, GM_WIDTH),
                   gm_ln_b[j].reshape(1, GM_WIDTH))
            tail = (gm_w_out[j].astype(BF16), ng(layer, 3), ng(layer, 4), ng(layer, 5),
                    win, wout, 2 * layer + 1)
            xp, = _odd(xp, *odd, wm.astype(BF16), bs, *tail, emit_v=False)
            xs, gv = _odd(xs, *odd, wm_s.astype(BF16), bs_s, *tail, emit_v=True)
            gv_s.append(gv.reshape(dbatch, dseq, GM_WIDTH))
    return (xp.reshape(batch, seq, D_MODEL), xs.reshape(dbatch, dseq, D_MODEL),
            jnp.stack(fk_p), jnp.stack(fv_p), jnp.stack(fl_p), jnp.stack(rs_p), jnp.stack(rsh_p),
            jnp.stack(fk_s), jnp.stack(fv_s), jnp.stack(fl_s), jnp.stack(rs_s), jnp.stack(rsh_s),
            jnp.stack(gv_s))
```
